```python
import math
import jax
import jax.numpy as jnp
from jax import lax
import numpy as np

D_MODEL = 1024
BATCH = 8
SEQ = 4096
DEPTH = 2
DEC_BATCH = 8
DEC_SEQ = 16
PAST_LEN = 1024

CHUNK = 64
N_A_LAYERS = DEPTH // 2
N_B_LAYERS = DEPTH - N_A_LAYERS
D_RNN = ((4 * D_MODEL // 3 + 127) // 128) * 128
N_RNN_BLOCKS = 16
RNN_BLOCK = D_RNN // N_RNN_BLOCKS
CONV_W = 4
LRU_C = 8.0
HEAD_DIM = 64
N_HEADS = D_MODEL // (2 * HEAD_DIM)
V_DIM = 2 * HEAD_DIM
ATTN_W = N_HEADS * V_DIM
Q_BLOCK = 128
ROPE_THETA = 10000.0
EPS = 1e-6

kernel_name = 'hawk_yoco_diff_attn_stream'


def _rmsnorm(x, g):
    xf = x.astype(jnp.float32)
    y = xf * lax.rsqrt(jnp.mean(xf * xf, axis=-1, keepdims=True) + EPS)
    return (y * g.astype(jnp.float32)).astype(x.dtype)


def _rope(x, pos):
    half = HEAD_DIM // 2
    inv = ROPE_THETA ** (-jnp.arange(half, dtype=jnp.float32) / half)
    ang = pos.astype(jnp.float32)[:, None] * inv[None, :]
    shape = (1, pos.shape[0]) + (1,) * (x.ndim - 3) + (half,)
    cos = jnp.cos(ang).reshape(shape)
    sin = jnp.sin(ang).reshape(shape)
    xf = x.astype(jnp.float32)
    x1, x2 = xf[..., :half], xf[..., half:]
    return jnp.concatenate([x1 * cos - x2 * sin, x2 * cos + x1 * sin], axis=-1).astype(x.dtype)


def _causal_conv(u, buf, w, b):
    t = u.shape[1]
    up = jnp.concatenate([buf.astype(u.dtype), u], axis=1)
    out = b + up[:, 0:t] * w[0]
    for j in range(1, CONV_W):
        out = out + up[:, j:j + t] * w[j]
    return out, up[:, t:]


def _block_diag(u, w, b):
    bsz, t, _ = u.shape
    ub = u.reshape(bsz, t, N_RNN_BLOCKS, RNN_BLOCK)
    return jnp.einsum('btni,nij->btnj', ub, w).reshape(bsz, t, D_RNN) + b


def _lin_comb(left, right):
    a1, b1 = left
    a2, b2 = right
    return a1 * a2, a2 * b1 + b2


def _rg_lru(u, h0, wr, br, wi, bi, lam):
    uf = u.astype(jnp.float32)
    r = jax.nn.sigmoid(_block_diag(u, wr, br).astype(jnp.float32))
    i = jax.nn.sigmoid(_block_diag(u, wi, bi).astype(jnp.float32))
    log_a = -LRU_C * r * jax.nn.softplus(-lam.astype(jnp.float32))
    a = jnp.exp(log_a)
    mult = jnp.sqrt(-jnp.expm1(2.0 * log_a))
    if h0 is None:
        mult = mult.at[:, 0].set(1.0)
        b = mult * (i * uf)
    else:
        b = mult * (i * uf)
        b = b.at[:, 0].add(a[:, 0] * h0.astype(jnp.float32))
    _, h = lax.associative_scan(_lin_comb, (a, b), axis=1)
    return h.astype(u.dtype), h[:, -1].astype(u.dtype)


def _a_layer(x, conv_buf, h0, norm_g, w_in, conv_w, conv_b, wr, br, wi, bi, lam, w_out):
    h = _rmsnorm(x, norm_g) @ w_in
    u, gate = h[..., :D_RNN], h[..., D_RNN:]
    u, new_buf = _causal_conv(u, conv_buf, conv_w, conv_b)
    y, h_last = _rg_lru(u, h0, wr, br, wi, bi, lam)
    return x + (y * jax.nn.silu(gate)) @ w_out, new_buf, h_last


def _shared_kv(x, pos, kv_norm, kv_w, k_norm):
    bsz, t, _ = x.shape
    kv = _rmsnorm(x, kv_norm) @ kv_w
    k = kv[..., :ATTN_W].reshape(bsz, t, N_HEADS, 2, HEAD_DIM)
    v = kv[..., ATTN_W:].reshape(bsz, t, N_HEADS, V_DIM)
    k = _rope(_rmsnorm(k, k_norm), pos).reshape(bsz, t, N_HEADS, V_DIM)
    return k, v


def _diff_attn(q, k, v, lam, mask):
    bsz, nk = k.shape[:2]
    k = k.reshape(bsz, nk, N_HEADS, 2, HEAD_DIM)
    s = jnp.einsum('bqhcd,bkhcd->bchqk', q, k).astype(jnp.float32) * (HEAD_DIM ** -0.5)
    if mask is not None:
        s = jnp.where(mask, s, -jnp.inf)
    p = jax.nn.softmax(s, axis=-1)
    pd = p[:, 0] - lam * p[:, 1]
    return jnp.einsum('bhqk,bkhe->bqhe', pd.astype(v.dtype), v)


def _prompt_attn(q, k, v, lam):
    bsz, t = q.shape[:2]
    nb = t // Q_BLOCK
    qb = q.reshape(bsz, nb, Q_BLOCK, N_HEADS, 2, HEAD_DIM).swapaxes(0, 1)
    kchunk = jnp.arange(t) // CHUNK

    def one(args):
        qblk, start = args
        qchunk = (start + jnp.arange(Q_BLOCK)) // CHUNK
        mask = kchunk[None, :] <= qchunk[:, None]
        return _diff_attn(qblk, k, v, lam, mask)

    o = lax.map(one, (qb, jnp.arange(nb) * Q_BLOCK))
    return o.swapaxes(0, 1).reshape(bsz, t, N_HEADS, V_DIM)


def _b_layer(x, pos, k, v, norm_g, w_in, q_norm, lq1, lk1, lq2, lk2, head_g, w_out, lam_init, prompt):
    bsz, t, _ = x.shape
    h = _rmsnorm(x, norm_g) @ w_in
    q, gate = h[..., :ATTN_W], h[..., ATTN_W:]
    q = _rope(_rmsnorm(q.reshape(bsz, t, N_HEADS, 2, HEAD_DIM), q_norm), pos)
    lam = (jnp.exp(jnp.sum(lq1.astype(jnp.float32) * lk1.astype(jnp.float32)))
           - jnp.exp(jnp.sum(lq2.astype(jnp.float32) * lk2.astype(jnp.float32))) + lam_init)
    if prompt:
        o = _prompt_attn(q, k, v, lam)
    else:
        o = _diff_attn(q, k, v, lam, None)
    o = _rmsnorm(o, head_g) * (1.0 - lam_init)
    o = o.reshape(bsz, t, ATTN_W) * jax.nn.silu(gate)
    return x + o @ w_out


def _run_group(x, pos, conv_bufs, h0s, past_k, past_v, p):
    prompt = past_k is None
    bsz = x.shape[0]
    new_bufs, new_hs = [], []
    k_new, v_new, k_all, v_all = None, None, None, None
    for i in range(DEPTH):
        if i < N_A_LAYERS:
            if prompt:
                buf = jnp.zeros((bsz, CONV_W - 1, D_RNN), x.dtype)
                h0 = None
            else:
                buf = conv_bufs[i]
                h0 = h0s[i]
            x, nb, nh = _a_layer(x, buf, h0, p['a_norm'][i], p['a_w_in'][i], p['a_conv_w'][i],
                                 p['a_conv_b'][i], p['a_gate_r_w'][i], p['a_gate_r_b'][i],
                                 p['a_gate_i_w'][i], p['a_gate_i_b'][i], p['a_lambda'][i],
                                 p['a_w_out'][i])
            new_bufs.append(nb)
            new_hs.append(nh)
        else:
            if i == N_A_LAYERS:
                k_new, v_new = _shared_kv(x, pos, p['kv_norm'], p['kv_w'], p['k_norm'])
                if prompt:
                    k_all, v_all = k_new, v_new
                else:
                    k_all = jnp.concatenate([past_k.astype(k_new.dtype), k_new], axis=1)
                    v_all = jnp.concatenate([past_v.astype(v_new.dtype), v_new], axis=1)
            j = i - N_A_LAYERS
            lam_init = 0.8 - 0.6 * math.exp(-0.3 * i)
            x = _b_layer(x, pos, k_all, v_all, p['b_norm'][j], p['b_w_in'][j], p['b_q_norm'][j],
                         p['b_lambda_q1'][j], p['b_lambda_k1'][j], p['b_lambda_q2'][j],
                         p['b_lambda_k2'][j], p['b_head_norm'][j], p['b_w_out'][j], lam_init, prompt)
    return x, jnp.stack(new_bufs), jnp.stack(new_hs), k_new, v_new


def _normal(k, shape, scale):
    return jax.random.normal(k, shape, jnp.float32) * scale


def setup_inputs(seed: int = 0) -> dict:
    key = jax.random.key(seed)
    ks = jax.random.split(key, 32)
    na, nbl = N_A_LAYERS, N_B_LAYERS
    u = jax.random.uniform(ks[14], (na, D_RNN), jnp.float32, minval=0.9, maxval=0.999)
    a0 = u ** (1.0 / LRU_C)
    a_lambda = jnp.log(a0) - jnp.log1p(-a0)
    return {
        'x_prompt': _normal(ks[0], (BATCH, SEQ, D_MODEL), 1.0),
        'x_sample': _normal(ks[1], (DEC_BATCH, DEC_SEQ, D_MODEL), 1.0),
        'state_conv': _normal(ks[2], (na, DEC_BATCH, CONV_W - 1, D_RNN), 1.0),
        'state_h': _normal(ks[3], (na, DEC_BATCH, D_RNN), 0.5),
        'cache_k': _normal(ks[4], (DEC_BATCH, PAST_LEN, N_HEADS, V_DIM), 1.0),
        'cache_v': _normal(ks[5], (DEC_BATCH, PAST_LEN, N_HEADS, V_DIM), 1.0),
        'a_norm': 1.0 + _normal(ks[6], (na, D_MODEL), 0.05),
        'a_w_in': _normal(ks[7], (na, D_MODEL, 2 * D_RNN), D_MODEL ** -0.5),
        'a_conv_w': _normal(ks[8], (na, CONV_W, D_RNN), 0.5),
        'a_conv_b': _normal(ks[9], (na, D_RNN), 0.02),
        'a_gate_r_w': _normal(ks[10], (na, N_RNN_BLOCKS, RNN_BLOCK, RNN_BLOCK), RNN_BLOCK ** -0.5),
        'a_gate_r_b': _normal(ks[11], (na, D_RNN), 0.02),
        'a_gate_i_w': _normal(ks[12], (na, N_RNN_BLOCKS, RNN_BLOCK, RNN_BLOCK), RNN_BLOCK ** -0.5),
        'a_gate_i_b': _normal(ks[13], (na, D_RNN), 0.02),
        'a_lambda': a_lambda,
        'a_w_out': _normal(ks[15], (na, D_RNN, D_MODEL), D_RNN ** -0.5),
        'kv_norm': 1.0 + _normal(ks[16], (D_MODEL,), 0.05),
        'kv_w': _normal(ks[17], (D_MODEL, 2 * ATTN_W), D_MODEL ** -0.5),
        'k_norm': 1.0 + _normal(ks[18], (HEAD_DIM,), 0.05),
        'b_norm': 1.0 + _normal(ks[19], (nbl, D_MODEL), 0.05),
        'b_w_in': _normal(ks[20], (nbl, D_MODEL, 2 * ATTN_W), D_MODEL ** -0.5),
        'b_q_norm': 1.0 + _normal(ks[21], (nbl, HEAD_DIM), 0.05),
        'b_lambda_q1': _normal(ks[22], (nbl, HEAD_DIM), 0.1),
        'b_lambda_k1': _normal(ks[23], (nbl, HEAD_DIM), 0.1),
        'b_lambda_q2': _normal(ks[24], (nbl, HEAD_DIM), 0.1),
        'b_lambda_k2': _normal(ks[25], (nbl, HEAD_DIM), 0.1),
        'b_head_norm': 1.0 + _normal(ks[26], (nbl, V_DIM), 0.05),
        'b_w_out': _normal(ks[27], (nbl, ATTN_W, D_MODEL), ATTN_W ** -0.5),
    }


def reference(x_prompt, x_sample, state_conv, state_h, cache_k, cache_v,
              a_norm, a_w_in, a_conv_w, a_conv_b, a_gate_r_w, a_gate_r_b, a_gate_i_w, a_gate_i_b,
              a_lambda, a_w_out, kv_norm, kv_w, k_norm, b_norm, b_w_in, b_q_norm,
              b_lambda_q1, b_lambda_k1, b_lambda_q2, b_lambda_k2, b_head_norm, b_w_out):
    p = dict(a_norm=a_norm, a_w_in=a_w_in, a_conv_w=a_conv_w, a_conv_b=a_conv_b,
             a_gate_r_w=a_gate_r_w, a_gate_r_b=a_gate_r_b, a_gate_i_w=a_gate_i_w,
             a_gate_i_b=a_gate_i_b, a_lambda=a_lambda, a_w_out=a_w_out, kv_norm=kv_norm,
             kv_w=kv_w, k_norm=k_norm, b_norm=b_norm, b_w_in=b_w_in, b_q_norm=b_q_norm,
             b_lambda_q1=b_lambda_q1, b_lambda_k1=b_lambda_k1, b_lambda_q2=b_lambda_q2,
             b_lambda_k2=b_lambda_k2, b_head_norm=b_head_norm, b_w_out=b_w_out)
    pos_p = jnp.arange(x_prompt.shape[1], dtype=jnp.int32)
    pos_s = cache_k.shape[1] + jnp.arange(x_sample.shape[1], dtype=jnp.int32)
    y_p, conv_p, h_p, k_p, v_p = _run_group(x_prompt, pos_p, None, None, None, None, p)
    y_s, conv_s, h_s, k_s, v_s = _run_group(x_sample, pos_s, state_conv, state_h, cache_k, cache_v, p)
    return (y_p, y_s, conv_p, h_p, k_p, v_p, conv_s, h_s, k_s, v_s)
```

```python
import functools
import math

import numpy as np
import jax
import jax.numpy as jnp
from jax import lax
from jax.experimental import pallas as pl
from jax.experimental.pallas import tpu as pltpu

LANE = 128
SUBLANE = 8
VMEM_LIMIT_BYTES = 56 * 1024 * 1024

CHUNK = 64
CONV_W = 4
LRU_C = 8.0
HEAD_DIM = 64
V_DIM = 2 * HEAD_DIM
ROPE_THETA = 10000.0
EPS = 1e-6
LOG2E = 1.4426950408889634
NEG_BIG = -1e30

F32 = jnp.float32
BF16 = jnp.bfloat16


def _sigmoid(x):
    return 1.0 / (1.0 + jnp.exp(-x))


def _gate_band_layout(d_rnn, blk):
    ncol = d_rnn // LANE
    spans = []
    for j in range(ncol):
        lo = ((LANE * j) // blk) * blk
        hi = ((LANE * j + LANE - 1) // blk + 1) * blk
        spans.append((lo, hi))
    width = max(-(-(hi - (lo // LANE) * LANE) // LANE) * LANE for lo, hi in spans)
    starts = []
    for lo, hi in spans:
        ws = min((lo // LANE) * LANE, d_rnn - width)
        assert ws <= lo and hi <= ws + width
        starts.append(ws)
    return tuple(starts), width


def _block_diag_dense(w):
    n, blk, _ = w.shape
    eye = jnp.eye(n, dtype=w.dtype)
    return jnp.einsum('nij,nm->nimj', w, eye).reshape(n * blk, n * blk)


def _rnn_layer_kernel(x_ref, conv0_ref, h0_ref, g_ref, w_in_ref, cw_ref, cb_ref, wg_ref, bg_ref,
                      lam_ref, w_out_ref, o_ref, conv_o_ref, h_o_ref,
                      ubuf, a_s, b_s, y_s, hc, *, reset, starts, gate_k):
    nb, tt, d = x_ref.shape
    rows = nb * tt
    d_rnn = lam_ref.shape[1]
    halo = (CONV_W - 1) * nb
    step = pl.program_id(0)

    @pl.when(step == 0)
    def _():
        ubuf[0:halo, :] = conv0_ref[...]
        hc[...] = h0_ref[...]

    x = pltpu.einshape("btc->tbc", x_ref[...]).reshape(rows, d)
    ms = jnp.mean(x * x, axis=-1, keepdims=True)
    xn = (x * lax.rsqrt(ms + EPS) * g_ref[...]).astype(BF16)
    h = jnp.dot(xn, w_in_ref[...], preferred_element_type=F32)
    ubuf[halo:halo + rows, :] = h[:, :d_rnn]
    gate = h[:, d_rnn:]

    u = cb_ref[...] + cw_ref[CONV_W - 1:CONV_W, :] * ubuf[halo:halo + rows, :]
    for j in range(CONV_W - 1):
        u = u + cw_ref[j:j + 1, :] * ubuf[j * nb:j * nb + rows, :]
    ubuf[0:halo, :] = ubuf[rows:rows + halo, :]
    conv_o_ref[...] = ubuf[0:halo, :]

    lam = lam_ref[...]
    softplus_neg = jnp.maximum(-lam, 0.0) + jnp.log1p(jnp.exp(-jnp.abs(lam)))
    c_lam = -LRU_C * softplus_neg

    u_bf = u.astype(BF16)
    if reset:
        row = lax.broadcasted_iota(jnp.int32, (rows, LANE), 0)
        first = row < jnp.where(step == 0, nb, 0)
    for j, ws in enumerate(starts):
        cs = slice(j * LANE, (j + 1) * LANE)
        ri = jnp.dot(u_bf[:, ws:ws + gate_k], wg_ref[j], preferred_element_type=F32) + bg_ref[j]
        r = _sigmoid(ri[:, :LANE])
        i = _sigmoid(ri[:, LANE:])
        a = jnp.exp(c_lam[:, cs] * r)
        mult = jnp.sqrt(1.0 - a * a)
        if reset:
            mult = jnp.where(first, 1.0, mult)
        a_s[:, cs] = a
        b_s[:, cs] = mult * (i * u[:, cs])

    def scan_step(t, hcur):
        r0 = pl.multiple_of(t * nb, SUBLANE)
        hn = a_s[pl.ds(r0, nb), :] * hcur + b_s[pl.ds(r0, nb), :]
        y_s[pl.ds(r0, nb), :] = hn
        return hn

    hlast = lax.fori_loop(0, tt, scan_step, hc[...], unroll=min(tt, 8))
    hc[...] = hlast
    h_o_ref[...] = hlast

    yg = (y_s[...] * (gate * _sigmoid(gate))).astype(BF16)
    out = x + jnp.dot(yg, w_out_ref[...], preferred_element_type=F32)
    o_ref[...] = pltpu.einshape("tbc->btc", out.reshape(tt, nb, d))


def _rnn_layer(x, conv0, h0, norm_g, w_in, conv_w, conv_b, wr, br, wi, bi, lam, w_out, *, reset, tt):
    nb, t, d = x.shape
    assert nb == SUBLANE and t % tt == 0 and tt % SUBLANE == 0
    d_rnn = lam.shape[-1]
    n_blk, blk, _ = wr.shape
    starts, gate_k = _gate_band_layout(d_rnn, blk)
    ncol = d_rnn // LANE
    rows = nb * tt
    halo = (CONV_W - 1) * nb

    dense_r = _block_diag_dense(wr)
    dense_i = _block_diag_dense(wi)
    wg = jnp.stack([
        jnp.concatenate([dense_r[ws:ws + gate_k, j * LANE:(j + 1) * LANE],
                         dense_i[ws:ws + gate_k, j * LANE:(j + 1) * LANE]], axis=1)
        for j, ws in enumerate(starts)]).astype(BF16)
    bg = jnp.concatenate([br.reshape(ncol, 1, LANE), bi.reshape(ncol, 1, LANE)], axis=2)

    full = lambda shape: pl.BlockSpec(shape, lambda i: (0,) * len(shape))
    kern = functools.partial(_rnn_layer_kernel, reset=reset, starts=starts, gate_k=gate_k)
    return pl.pallas_call(
        kern,
        grid=(t // tt,),
        in_specs=[
            pl.BlockSpec((nb, tt, d), lambda i: (0, i, 0)),
            full((halo, d_rnn)), full((nb, d_rnn)), full((1, d)),
            full((d, 2 * d_rnn)), full((CONV_W, d_rnn)), full((1, d_rnn)),
            full((ncol, gate_k, 2 * LANE)), full((ncol, 1, 2 * LANE)), full((1, d_rnn)),
            full((d_rnn, d)),
        ],
        out_specs=[
            pl.BlockSpec((nb, tt, d), lambda i: (0, i, 0)),
            full((halo, d_rnn)), full((nb, d_rnn)),
        ],
        out_shape=[
            jax.ShapeDtypeStruct((nb, t, d), F32),
            jax.ShapeDtypeStruct((halo, d_rnn), F32),
            jax.ShapeDtypeStruct((nb, d_rnn), F32),
        ],
        scratch_shapes=[
            pltpu.VMEM((halo + rows, d_rnn), F32),
            pltpu.VMEM((rows, d_rnn), F32),
            pltpu.VMEM((rows, d_rnn), F32),
            pltpu.VMEM((rows, d_rnn), F32),
            pltpu.VMEM((nb, d_rnn), F32),
        ],
        compiler_params=pltpu.CompilerParams(
            dimension_semantics=("arbitrary",), vmem_limit_bytes=VMEM_LIMIT_BYTES),
        name="rnn_layer_reset" if reset else "rnn_layer_state",
    )(x, conv0, h0, norm_g.reshape(1, d), w_in.astype(BF16), conv_w, conv_b.reshape(1, d_rnn),
      wg, bg, lam.reshape(1, d_rnn), w_out.astype(BF16))


def _group_sum_matrices(width):
    groups = width // HEAD_DIM
    assert groups <= LANE
    gather = np.zeros((width, LANE), np.float32)
    gather[np.arange(width), np.arange(width) // HEAD_DIM] = 1.0
    expand = np.concatenate([gather.T, gather.T], axis=0)
    return jnp.asarray(gather, BF16), jnp.asarray(expand, BF16)


def _group_rms_scale(x, gather, expand):
    ssum = jnp.dot((x * x).astype(BF16), gather, preferred_element_type=F32)
    inv = lax.rsqrt(ssum * (1.0 / HEAD_DIM) + EPS)
    hi = inv.astype(BF16)
    lo = (inv - hi.astype(F32)).astype(BF16)
    return jnp.dot(jnp.concatenate([hi, lo], axis=1), expand, preferred_element_type=F32)


def _rope_slab(xh, cos_t, sin_t):
    lane = lax.broadcasted_iota(jnp.int32, xh.shape, 1)
    first_half = (lane % HEAD_DIM) < (HEAD_DIM // 2)
    partner = jnp.where(first_half, pltpu.roll(xh, LANE - HEAD_DIM // 2, 1),
                        pltpu.roll(xh, HEAD_DIM // 2, 1))
    return xh * cos_t + partner * sin_t


def _proj_kernel(x_ref, gkv_ref, gb_ref, kvw_ref, bw_ref, gk_ref, gq_ref, cos_ref, sin_ref,
                 gather_ref, expand_ref, *out_refs, prompt, kt):
    if prompt:
        k_ref, v_ref, khm_ref, vt_ref, qt_ref, sg_ref = out_refs
    else:
        k_ref, v_ref, q_ref, sg_ref = out_refs
    x = x_ref[0]
    rows, d = x.shape
    w = gk_ref.shape[1]
    nh = w // V_DIM
    ms = jnp.mean(x * x, axis=-1, keepdims=True)
    xh = x * lax.rsqrt(ms + EPS)
    kv = jnp.dot((xh * gkv_ref[...]).astype(BF16), kvw_ref[...], preferred_element_type=F32)
    qg = jnp.dot((xh * gb_ref[...]).astype(BF16), bw_ref[...], preferred_element_type=F32)
    k_raw, v = kv[:, :w], kv[:, w:]
    q_raw, gate = qg[:, :w], qg[:, w:]
    gather, expand = gather_ref[...], expand_ref[...]
    kn = k_raw * _group_rms_scale(k_raw, gather, expand) * gk_ref[...]
    qn = q_raw * _group_rms_scale(q_raw, gather, expand) * gq_ref[...]
    sg = gate * _sigmoid(gate)
    cos_t, sin_t = cos_ref[...], sin_ref[...]
    v_ref[0] = v
    if not prompt:
        sg_ref[0] = sg
    for h in range(nh):
        cs = slice(h * V_DIM, (h + 1) * V_DIM)
        kh = _rope_slab(kn[:, cs], cos_t, sin_t)
        qh = _rope_slab(qn[:, cs], cos_t, sin_t)
        k_ref[0, :, cs] = kh
        if prompt:
            khm_ref[0, h] = kh.astype(BF16)
            sg_ref[0, h] = sg[:, cs].astype(BF16)
            for c in range(rows // kt):
                rs = slice(c * kt, (c + 1) * kt)
                vt_ref[0, h, c] = v[rs, cs].T.astype(BF16)
                qt_ref[0, h, c] = qh[rs, :].T.astype(BF16)
        else:
            q_ref[0, :, cs] = qh


def _rope_tables(pos):
    half = HEAD_DIM // 2
    inv = ROPE_THETA ** (-jnp.arange(half, dtype=F32) / half)
    ang = pos.astype(F32)[:, None] * inv[None, :]
    cos, sin = jnp.cos(ang), jnp.sin(ang)
    reps = LANE // HEAD_DIM
    return (jnp.tile(jnp.concatenate([cos, cos], axis=1), (1, reps)),
            jnp.tile(jnp.concatenate([-sin, sin], axis=1), (1, reps)))


def _proj(x, pos, kv_norm, b_norm, kv_w, b_w_in, k_norm, q_norm, *, prompt, rows, kt):
    nb, t, d = x.shape
    w = kv_w.shape[1] // 2
    nh = w // V_DIM
    assert t % rows == 0 and (not prompt or rows % kt == 0)
    cos_t, sin_t = _rope_tables(pos)
    gather, expand = _group_sum_matrices(w)
    reps = w // HEAD_DIM
    gk = jnp.tile(k_norm.astype(F32), reps).reshape(1, w)
    gq = (jnp.tile(q_norm.astype(F32), reps) * (HEAD_DIM ** -0.5 * LOG2E)).reshape(1, w)

    full = lambda shape: pl.BlockSpec(shape, lambda b, i: (0,) * len(shape))
    tok = pl.BlockSpec((1, rows, w), lambda b, i: (b, i, 0))
    out_specs = [tok, tok]
    out_shape = [jax.ShapeDtypeStruct((nb, t, w), F32)] * 2
    if prompt:
        nc = rows // kt
        hm = pl.BlockSpec((1, nh, rows, V_DIM), lambda b, i: (b, 0, i, 0))
        tr = pl.BlockSpec((1, nh, nc, V_DIM, kt), lambda b, i: (b, 0, i, 0, 0))
        out_specs += [hm, tr, tr, hm]
        out_shape += [jax.ShapeDtypeStruct((nb, nh, t, V_DIM), BF16),
                      jax.ShapeDtypeStruct((nb, nh, t // kt, V_DIM, kt), BF16),
                      jax.ShapeDtypeStruct((nb, nh, t // kt, V_DIM, kt), BF16),
                      jax.ShapeDtypeStruct((nb, nh, t, V_DIM), BF16)]
    else:
        out_specs += [tok, tok]
        out_shape += [jax.ShapeDtypeStruct((nb, t, w), F32)] * 2
    return pl.pallas_call(
        functools.partial(_proj_kernel, prompt=prompt, kt=kt),
        grid=(nb, t // rows),
        in_specs=[
            pl.BlockSpec((1, rows, d), lambda b, i: (b, i, 0)),
            full((1, d)), full((1, d)), full((d, 2 * w)), full((d, 2 * w)),
            full((1, w)), full((1, w)),
            pl.BlockSpec((rows, LANE), lambda b, i: (i, 0)),
            pl.BlockSpec((rows, LANE), lambda b, i: (i, 0)),
            full((w, LANE)), full((2 * LANE, w)),
        ],
        out_specs=out_specs,
        out_shape=out_shape,
        compiler_params=pltpu.CompilerParams(
            dimension_semantics=("parallel", "parallel"), vmem_limit_bytes=VMEM_LIMIT_BYTES),
        name="proj_prompt" if prompt else "proj_sample",
    )(x, kv_norm.reshape(1, d), b_norm.reshape(1, d), kv_w.astype(BF16), b_w_in.astype(BF16),
      gk, gq, cos_t, sin_t, gather, expand)


def _diff_lambda(lq1, lk1, lq2, lk2, lam_init):
    return (jnp.exp(jnp.sum(lq1 * lk1, axis=-1, keepdims=True))
            - jnp.exp(jnp.sum(lq2 * lk2, axis=-1, keepdims=True)) + lam_init)


def _prompt_attn_kernel(qt_ref, khm_ref, vt_ref, sg_ref, x_ref, w_out_ref, hg_ref,
                        lq1_ref, lk1_ref, lq2_ref, lk2_ref, o_ref, oh_s, *, lam_init):
    nh, _, vd, tq = qt_ref.shape[1], qt_ref.shape[2], qt_ref.shape[3], qt_ref.shape[4]
    qi = pl.program_id(1)
    lam = _diff_lambda(lq1_ref[...], lk1_ref[...], lq2_ref[...], lk2_ref[...], lam_init)
    feat = lax.broadcasted_iota(jnp.int32, (vd, tq), 0)
    krow = lax.broadcasted_iota(jnp.int32, (tq, tq), 0)
    qcol = lax.broadcasted_iota(jnp.int32, (tq, tq), 1)
    visible = (krow // CHUNK) <= (qcol // CHUNK)

    def head_body(h, carry):
        qt = qt_ref[0, h, 0]
        zero = jnp.zeros_like(qt)
        q_maps = (jnp.where(feat < HEAD_DIM, qt, zero), jnp.where(feat >= HEAD_DIM, qt, zero))

        def tile_update(kt, state, masked):
            k_t = khm_ref[0, h, pl.ds(pl.multiple_of(kt * tq, tq), tq), :]
            vt_t = vt_ref[0, h, kt]
            new = []
            for (m, l, acc), q_c in zip(state, q_maps):
                s = jnp.dot(k_t, q_c, preferred_element_type=F32)
                if masked:
                    s = jnp.where(visible, s, NEG_BIG)
                m_new = jnp.maximum(m, jnp.max(s, axis=0, keepdims=True))
                alpha = jnp.exp2(m - m_new)
                p = jnp.exp2(s - m_new)
                l_new = alpha * l + jnp.sum(p, axis=0, keepdims=True)
                acc_new = alpha * acc + jnp.dot(vt_t, p.astype(BF16), preferred_element_type=F32)
                new.append((m_new, l_new, acc_new))
            return tuple(new)

        init = tuple((jnp.full((1, tq), NEG_BIG, F32), jnp.zeros((1, tq), F32),
                      jnp.zeros((vd, tq), F32)) for _ in range(2))
        state = lax.fori_loop(0, qi, lambda kt, st: tile_update(kt, st, False), init)
        (_, l1, acc1), (_, l2, acc2) = tile_update(qi, state, True)
        ot = acc1 / l1 - lam * (acc2 / l2)
        ms = jnp.mean(ot * ot, axis=0, keepdims=True)
        on = ot * lax.rsqrt(ms + EPS) * hg_ref[...] * (1.0 - lam_init)
        oh_s[h] = (on.T * sg_ref[0, h].astype(F32)).astype(BF16)
        return carry

    lax.fori_loop(0, nh, head_body, 0)
    o = jnp.concatenate([oh_s[h] for h in range(nh)], axis=1)
    o_ref[0] = x_ref[0] + jnp.dot(o, w_out_ref[...], preferred_element_type=F32)


def _prompt_attn(qt, khm, vt, sg, x, w_out, head_g, lq1, lk1, lq2, lk2, *, lam_init):
    nb, nh, nq, vd, tq = qt.shape
    t = khm.shape[2]
    d = x.shape[-1]
    w = nh * vd
    hg = jnp.broadcast_to(head_g.astype(F32)[:, None], (vd, tq))
    full = lambda shape: pl.BlockSpec(shape, lambda b, i: (0,) * len(shape))
    row = lambda a: a.reshape(1, -1).astype(F32)
    return pl.pallas_call(
        functools.partial(_prompt_attn_kernel, lam_init=lam_init),
        grid=(nb, nq),
        in_specs=[
            pl.BlockSpec((1, nh, 1, vd, tq), lambda b, i: (b, 0, i, 0, 0)),
            pl.BlockSpec((1, nh, t, vd), lambda b, i: (b, 0, 0, 0)),
            pl.BlockSpec((1, nh, nq, vd, tq), lambda b, i: (b, 0, 0, 0, 0)),
            pl.BlockSpec((1, nh, tq, vd), lambda b, i: (b, 0, i, 0)),
            pl.BlockSpec((1, tq, d), lambda b, i: (b, i, 0)),
            full((w, d)), full((vd, tq)),
            full((1, HEAD_DIM)), full((1, HEAD_DIM)), full((1, HEAD_DIM)), full((1, HEAD_DIM)),
        ],
        out_specs=pl.BlockSpec((1, tq, d), lambda b, i: (b, i, 0)),
        out_shape=jax.ShapeDtypeStruct((nb, t, d), F32),
        scratch_shapes=[pltpu.VMEM((nh, tq, vd), BF16)],
        compiler_params=pltpu.CompilerParams(
            dimension_semantics=("parallel", "arbitrary"), vmem_limit_bytes=VMEM_LIMIT_BYTES),
        name="attn_prompt",
    )(qt, khm, vt, sg, x, w_out.astype(BF16), hg, row(lq1), row(lk1), row(lq2), row(lk2))


def _sample_attn_kernel(q_ref, kn_ref, vn_ref, ck_ref, cv_ref, sg_ref, x_ref, w_out_ref, hg_ref,
                        lq1_ref, lk1_ref, lq2_ref, lk2_ref, o_ref, *, lam_init):
    q = q_ref[0]
    tq, w = q.shape
    nh = w // V_DIM
    lam = _diff_lambda(lq1_ref[...], lk1_ref[...], lq2_ref[...], lk2_ref[...], lam_init)
    lane = lax.broadcasted_iota(jnp.int32, (tq, V_DIM), 1)
    contract_last = (((1,), (1,)), ((), ()))
    outs = []
    for h in range(nh):
        cs = slice(h * V_DIM, (h + 1) * V_DIM)
        qh = q[:, cs]
        ck = ck_ref[0, :, cs].astype(BF16)
        cv = cv_ref[0, :, cs].astype(BF16)
        kn = kn_ref[0, :, cs].astype(BF16)
        vn = vn_ref[0, :, cs].astype(BF16)
        oh = None
        for c in range(2):
            in_map = (lane >= HEAD_DIM) if c else (lane < HEAD_DIM)
            q_c = jnp.where(in_map, qh, 0.0).astype(BF16)
            s_old = lax.dot_general(q_c, ck, contract_last, preferred_element_type=F32)
            s_new = lax.dot_general(q_c, kn, contract_last, preferred_element_type=F32)
            m = jnp.maximum(jnp.max(s_old, axis=-1, keepdims=True),
                            jnp.max(s_new, axis=-1, keepdims=True))
            p_old = jnp.exp2(s_old - m)
            p_new = jnp.exp2(s_new - m)
            l = jnp.sum(p_old, axis=-1, keepdims=True) + jnp.sum(p_new, axis=-1, keepdims=True)
            pv = (jnp.dot(p_old.astype(BF16), cv, preferred_element_type=F32)
                  + jnp.dot(p_new.astype(BF16), vn, preferred_element_type=F32)) / l
            oh = pv if c == 0 else oh - lam * pv
        ms = jnp.mean(oh * oh, axis=-1, keepdims=True)
        outs.append(oh * lax.rsqrt(ms + EPS) * hg_ref[...] * (1.0 - lam_init))
    o = (jnp.concatenate(outs, axis=1) * sg_ref[0]).astype(BF16)
    o_ref[0] = x_ref[0] + jnp.dot(o, w_out_ref[...], preferred_element_type=F32)


def _sample_attn(q, k_new, v_new, cache_k, cache_v, sg, x, w_out, head_g, lq1, lk1, lq2, lk2, *, lam_init):
    nb, tq, w = q.shape
    past = cache_k.shape[1]
    d = x.shape[-1]
    full = lambda shape: pl.BlockSpec(shape, lambda b: (0,) * len(shape))
    tok = lambda width: pl.BlockSpec((1, tq, width), lambda b: (b, 0, 0))
    old = pl.BlockSpec((1, past, w), lambda b: (b, 0, 0))
    row = lambda a: a.reshape(1, -1).astype(F32)
    return pl.pallas_call(
        functools.partial(_sample_attn_kernel, lam_init=lam_init),
        grid=(nb,),
        in_specs=[tok(w), tok(w), tok(w), old, old, tok(w), tok(d),
                  full((w, d)), full((1, V_DIM)),
                  full((1, HEAD_DIM)), full((1, HEAD_DIM)), full((1, HEAD_DIM)), full((1, HEAD_DIM))],
        out_specs=tok(d),
        out_shape=jax.ShapeDtypeStruct((nb, tq, d), F32),
        compiler_params=pltpu.CompilerParams(
            dimension_semantics=("parallel",), vmem_limit_bytes=VMEM_LIMIT_BYTES),
        name="attn_sample",
    )(q, k_new, v_new, cache_k.reshape(nb, past, w), cache_v.reshape(nb, past, w), sg, x,
      w_out.astype(BF16), row(head_g), row(lq1), row(lk1), row(lq2), row(lk2))


def _pick_tile(n, target):
    tile = min(n, target)
    while n % tile:
        tile //= 2
    return tile


def kernel(x_prompt, x_sample, state_conv, state_h, cache_k, cache_v, a_norm, a_w_in, a_conv_w, a_conv_b, a_gate_r_w, a_gate_r_b, a_gate_i_w, a_gate_i_b, a_lambda, a_w_out, kv_norm, kv_w, k_norm, b_norm, b_w_in, b_q_norm, b_lambda_q1, b_lambda_k1, b_lambda_q2, b_lambda_k2, b_head_norm, b_w_out):
    n_a, n_b = a_norm.shape[0], b_norm.shape[0]
    assert n_a == 1 and n_b == 1, "one recurrent layer followed by one attention layer"
    nb, t, d = x_prompt.shape
    nbs, ts, _ = x_sample.shape
    d_rnn = a_lambda.shape[-1]
    w = kv_w.shape[1] // 2
    nh = w // V_DIM
    past = cache_k.shape[1]
    halo = CONV_W - 1
    lam_init = 0.8 - 0.6 * math.exp(-0.3 * n_a)
    a_args = (a_norm[0], a_w_in[0], a_conv_w[0], a_conv_b[0], a_gate_r_w[0], a_gate_r_b[0],
              a_gate_i_w[0], a_gate_i_b[0], a_lambda[0], a_w_out[0])
    lam_args = (b_lambda_q1[0], b_lambda_k1[0], b_lambda_q2[0], b_lambda_k2[0])

    def conv_state_out(c, n):
        return jnp.swapaxes(c.reshape(halo, n, d_rnn), 0, 1)[None]

    x1, conv_p, h_p = _rnn_layer(
        x_prompt, jnp.zeros((halo * nb, d_rnn), F32), jnp.zeros((nb, d_rnn), F32), *a_args,
        reset=True, tt=_pick_tile(t, 64))
    tq = _pick_tile(t, 256)
    k_p, v_p, khm, vt, qt, sg = _proj(
        x1, jnp.arange(t, dtype=jnp.int32), kv_norm, b_norm[0], kv_w, b_w_in[0], k_norm, b_q_norm[0],
        prompt=True, rows=_pick_tile(t, 512), kt=tq)
    y_p = _prompt_attn(qt, khm, vt, sg, x1, b_w_out[0], b_head_norm[0], *lam_args, lam_init=lam_init)

    conv0 = jnp.swapaxes(state_conv[0], 0, 1).reshape(halo * nbs, d_rnn)
    x1s, conv_s, h_s = _rnn_layer(x_sample, conv0, state_h[0], *a_args, reset=False, tt=ts)
    k_s, v_s, q_s, sg_s = _proj(
        x1s, past + jnp.arange(ts, dtype=jnp.int32), kv_norm, b_norm[0], kv_w, b_w_in[0], k_norm,
        b_q_norm[0], prompt=False, rows=ts, kt=ts)
    y_s = _sample_attn(q_s, k_s, v_s, cache_k, cache_v, sg_s, x1s, b_w_out[0], b_head_norm[0],
                       *lam_args, lam_init=lam_init)

    return (y_p, y_s,
            conv_state_out(conv_p, nb), h_p[None],
            k_p.reshape(nb, t, nh, V_DIM), v_p.reshape(nb, t, nh, V_DIM),
            conv_state_out(conv_s, nbs), h_s[None],
            k_s.reshape(nbs, ts, nh, V_DIM), v_s.reshape(nbs, ts, nh, V_DIM))
```

```python
import functools
import math

import numpy as np
import jax
import jax.numpy as jnp
from jax import lax
from jax.experimental import pallas as pl
from jax.experimental.pallas import tpu as pltpu

LANE = 128
SUBLANE = 8
VMEM_LIMIT_BYTES = 56 * 1024 * 1024

CHUNK = 64
CONV_W = 4
LRU_C = 8.0
HEAD_DIM = 64
V_DIM = 2 * HEAD_DIM
ROPE_THETA = 10000.0
EPS = 1e-6
LOG2E = 1.4426950408889634
NEG_BIG = -1e30
BF16_ROWS = 16
KEY_NORM_SLACK = 1.0 + 2.0 ** -6
DENOM_FLOOR = 2.0 ** -100

F32 = jnp.float32
BF16 = jnp.bfloat16


def _sigmoid(x):
    return 1.0 / (1.0 + jnp.exp(-x))


def _gate_band_layout(d_rnn, blk):
    ncol = d_rnn // LANE
    spans = []
    for j in range(ncol):
        lo = ((LANE * j) // blk) * blk
        hi = ((LANE * j + LANE - 1) // blk + 1) * blk
        spans.append((lo, hi))
    width = max(-(-(hi - (lo // LANE) * LANE) // LANE) * LANE for lo, hi in spans)
    starts = []
    for lo, hi in spans:
        ws = min((lo // LANE) * LANE, d_rnn - width)
        assert ws <= lo and hi <= ws + width
        starts.append(ws)
    return tuple(starts), width


def _block_diag_dense(w):
    n, blk, _ = w.shape
    eye = jnp.eye(n, dtype=w.dtype)
    return jnp.einsum('nij,nm->nimj', w, eye).reshape(n * blk, n * blk)


def _rnn_layer_kernel(x_ref, conv0_ref, h0_ref, g_ref, w_in_ref, cw_ref, cb_ref, wg_ref, bg_ref,
                      lam_ref, w_out_ref, o_ref, conv_o_ref, h_o_ref,
                      ubuf, a_s, b_s, y_s, hc, *, reset, starts, gate_k):
    nb, tt, d = x_ref.shape
    rows = nb * tt
    d_rnn = lam_ref.shape[1]
    halo = (CONV_W - 1) * nb
    step = pl.program_id(0)

    @pl.when(step == 0)
    def _():
        ubuf[0:halo, :] = conv0_ref[...]
        hc[...] = h0_ref[...]

    x = jnp.swapaxes(x_ref[...], 0, 1).reshape(rows, d)
    ms = jnp.mean(x * x, axis=-1, keepdims=True)
    xn = (x * lax.rsqrt(ms + EPS) * g_ref[...]).astype(BF16)
    h = jnp.dot(xn, w_in_ref[...], preferred_element_type=F32)
    ubuf[halo:halo + rows, :] = h[:, :d_rnn]
    gate = h[:, d_rnn:]

    u = cb_ref[...] + cw_ref[CONV_W - 1:CONV_W, :] * ubuf[halo:halo + rows, :]
    for j in range(CONV_W - 1):
        u = u + cw_ref[j:j + 1, :] * ubuf[j * nb:j * nb + rows, :]
    ubuf[0:halo, :] = ubuf[rows:rows + halo, :]
    conv_o_ref[...] = ubuf[0:halo, :]

    lam = lam_ref[...]
    softplus_neg = jnp.maximum(-lam, 0.0) + jnp.log1p(jnp.exp(-jnp.abs(lam)))
    c_lam = -LRU_C * softplus_neg

    u_bf = u.astype(BF16)
    if reset:
        row = lax.broadcasted_iota(jnp.int32, (rows, LANE), 0)
        first = row < jnp.where(step == 0, nb, 0)
    for j, ws in enumerate(starts):
        cs = slice(j * LANE, (j + 1) * LANE)
        ri = jnp.dot(u_bf[:, ws:ws + gate_k], wg_ref[j], preferred_element_type=F32) + bg_ref[j]
        r = _sigmoid(ri[:, :LANE])
        i = _sigmoid(ri[:, LANE:])
        a = jnp.exp(c_lam[:, cs] * r)
        mult = jnp.sqrt(1.0 - a * a)
        if reset:
            mult = jnp.where(first, 1.0, mult)
        a_s[:, cs] = a
        b_s[:, cs] = mult * (i * u[:, cs])

    def scan_step(t, hcur):
        r0 = pl.multiple_of(t * nb, SUBLANE)
        hn = a_s[pl.ds(r0, nb), :] * hcur + b_s[pl.ds(r0, nb), :]
        y_s[pl.ds(r0, nb), :] = hn
        return hn

    hlast = lax.fori_loop(0, tt, scan_step, hc[...], unroll=min(tt, 8))
    hc[...] = hlast
    h_o_ref[...] = hlast

    yg = (y_s[...] * (gate * _sigmoid(gate))).astype(BF16)
    out = x + jnp.dot(yg, w_out_ref[...], preferred_element_type=F32)
    o_ref[...] = jnp.swapaxes(out.reshape(tt, nb, d), 0, 1)


def _rnn_layer(x, conv0, h0, norm_g, w_in, conv_w, conv_b, wr, br, wi, bi, lam, w_out, *, reset, tt):
    nb, t, d = x.shape
    assert nb == SUBLANE and t % tt == 0 and tt % SUBLANE == 0
    d_rnn = lam.shape[-1]
    n_blk, blk, _ = wr.shape
    starts, gate_k = _gate_band_layout(d_rnn, blk)
    ncol = d_rnn // LANE
    rows = nb * tt
    halo = (CONV_W - 1) * nb

    dense_r = _block_diag_dense(wr)
    dense_i = _block_diag_dense(wi)
    wg = jnp.stack([
        jnp.concatenate([dense_r[ws:ws + gate_k, j * LANE:(j + 1) * LANE],
                         dense_i[ws:ws + gate_k, j * LANE:(j + 1) * LANE]], axis=1)
        for j, ws in enumerate(starts)]).astype(BF16)
    bg = jnp.concatenate([br.reshape(ncol, 1, LANE), bi.reshape(ncol, 1, LANE)], axis=2)

    full = lambda shape: pl.BlockSpec(shape, lambda i: (0,) * len(shape))
    kern = functools.partial(_rnn_layer_kernel, reset=reset, starts=starts, gate_k=gate_k)
    return pl.pallas_call(
        kern,
        grid=(t // tt,),
        in_specs=[
            pl.BlockSpec((nb, tt, d), lambda i: (0, i, 0)),
            full((halo, d_rnn)), full((nb, d_rnn)), full((1, d)),
            full((d, 2 * d_rnn)), full((CONV_W, d_rnn)), full((1, d_rnn)),
            full((ncol, gate_k, 2 * LANE)), full((ncol, 1, 2 * LANE)), full((1, d_rnn)),
            full((d_rnn, d)),
        ],
        out_specs=[
            pl.BlockSpec((nb, tt, d), lambda i: (0, i, 0)),
            full((halo, d_rnn)), full((nb, d_rnn)),
        ],
        out_shape=[
            jax.ShapeDtypeStruct((nb, t, d), F32),
            jax.ShapeDtypeStruct((halo, d_rnn), F32),
            jax.ShapeDtypeStruct((nb, d_rnn), F32),
        ],
        scratch_shapes=[
            pltpu.VMEM((halo + rows, d_rnn), F32),
            pltpu.VMEM((rows, d_rnn), F32),
            pltpu.VMEM((rows, d_rnn), F32),
            pltpu.VMEM((rows, d_rnn), F32),
            pltpu.VMEM((nb, d_rnn), F32),
        ],
        compiler_params=pltpu.CompilerParams(
            dimension_semantics=("arbitrary",), vmem_limit_bytes=VMEM_LIMIT_BYTES),
        name="rnn_layer_reset" if reset else "rnn_layer_state",
    )(x, conv0, h0, norm_g.reshape(1, d), w_in.astype(BF16), conv_w, conv_b.reshape(1, d_rnn),
      wg, bg, lam.reshape(1, d_rnn), w_out.astype(BF16))


def _group_sum_matrices(width):
    groups = width // HEAD_DIM
    assert groups <= LANE
    gather = np.zeros((width, LANE), np.float32)
    gather[np.arange(width), np.arange(width) // HEAD_DIM] = 1.0
    expand = np.concatenate([gather.T, gather.T], axis=0)
    return jnp.asarray(gather, BF16), jnp.asarray(expand, BF16)


def _group_rms_scale(x, gather, expand):
    ssum = jnp.dot((x * x).astype(BF16), gather, preferred_element_type=F32)
    inv = lax.rsqrt(ssum * (1.0 / HEAD_DIM) + EPS)
    hi = inv.astype(BF16)
    lo = (inv - hi.astype(F32)).astype(BF16)
    return jnp.dot(jnp.concatenate([hi, lo], axis=1), expand, preferred_element_type=F32)


def _rope_slab(xh, cos_t, sin_t):
    lane = lax.broadcasted_iota(jnp.int32, xh.shape, 1)
    first_half = (lane % HEAD_DIM) < (HEAD_DIM // 2)
    partner = jnp.where(first_half, pltpu.roll(xh, LANE - HEAD_DIM // 2, 1),
                        pltpu.roll(xh, HEAD_DIM // 2, 1))
    return xh * cos_t + partner * sin_t


def _proj_kernel(x_ref, gkv_ref, gb_ref, kvw_ref, bw_ref, gk_ref, gq_ref, cos_ref, sin_ref,
                 gather_ref, expand_ref, *out_refs, prompt, kt):
    if prompt:
        k_ref, v_ref, khm_ref, vt_ref, qt_ref, sg_ref = out_refs
    else:
        k_ref, v_ref, q_ref, sg_ref = out_refs
    x = x_ref[0]
    rows, d = x.shape
    w = gk_ref.shape[1]
    nh = w // V_DIM
    ms = jnp.mean(x * x, axis=-1, keepdims=True)
    xh = x * lax.rsqrt(ms + EPS)
    kv = jnp.dot((xh * gkv_ref[...]).astype(BF16), kvw_ref[...], preferred_element_type=F32)
    qg = jnp.dot((xh * gb_ref[...]).astype(BF16), bw_ref[...], preferred_element_type=F32)
    k_raw, v = kv[:, :w], kv[:, w:]
    q_raw, gate = qg[:, :w], qg[:, w:]
    gather, expand = gather_ref[...], expand_ref[...]
    kn = k_raw * _group_rms_scale(k_raw, gather, expand) * gk_ref[...]
    qn = q_raw * _group_rms_scale(q_raw, gather, expand) * gq_ref[...]
    sg = gate * _sigmoid(gate)
    cos_t, sin_t = cos_ref[...], sin_ref[...]
    v_ref[0] = v
    if prompt:
        pad_row = lax.broadcasted_iota(jnp.int32, (BF16_ROWS, kt), 0)
        ones_row = jnp.where(pad_row == 0, 1.0, 0.0).astype(BF16)
    else:
        sg_ref[0] = sg
    for h in range(nh):
        cs = slice(h * V_DIM, (h + 1) * V_DIM)
        kh = _rope_slab(kn[:, cs], cos_t, sin_t)
        qh = _rope_slab(qn[:, cs], cos_t, sin_t)
        k_ref[0, :, cs] = kh
        if prompt:
            khm_ref[0, h] = kh.astype(BF16)
            sg_ref[0, h] = sg[:, cs].astype(BF16)
            for c in range(rows // kt):
                rs = slice(c * kt, (c + 1) * kt)
                vt_ref[0, h, c, 0:V_DIM, :] = v[rs, cs].T.astype(BF16)
                vt_ref[0, h, c, V_DIM:V_DIM + BF16_ROWS, :] = ones_row
                qt_ref[0, h, c] = qh[rs, :].T.astype(BF16)
        else:
            q_ref[0, :, cs] = qh


def _rope_tables(pos):
    half = HEAD_DIM // 2
    inv = ROPE_THETA ** (-jnp.arange(half, dtype=F32) / half)
    ang = pos.astype(F32)[:, None] * inv[None, :]
    cos, sin = jnp.cos(ang), jnp.sin(ang)
    reps = LANE // HEAD_DIM
    return (jnp.tile(jnp.concatenate([cos, cos], axis=1), (1, reps)),
            jnp.tile(jnp.concatenate([-sin, sin], axis=1), (1, reps)))


def _proj(x, pos, kv_norm, b_norm, kv_w, b_w_in, k_norm, q_norm, *, prompt, rows, kt):
    nb, t, d = x.shape
    w = kv_w.shape[1] // 2
    nh = w // V_DIM
    assert t % rows == 0 and (not prompt or rows % kt == 0)
    cos_t, sin_t = _rope_tables(pos)
    gather, expand = _group_sum_matrices(w)
    reps = w // HEAD_DIM
    gk = jnp.tile(k_norm.astype(F32), reps).reshape(1, w)
    gq = (jnp.tile(q_norm.astype(F32), reps) * (HEAD_DIM ** -0.5 * LOG2E)).reshape(1, w)

    full = lambda shape: pl.BlockSpec(shape, lambda b, i: (0,) * len(shape))
    tok = pl.BlockSpec((1, rows, w), lambda b, i: (b, i, 0))
    out_specs = [tok, tok]
    out_shape = [jax.ShapeDtypeStruct((nb, t, w), F32)] * 2
    if prompt:
        nc = rows // kt
        hm = pl.BlockSpec((1, nh, rows, V_DIM), lambda b, i: (b, 0, i, 0))
        tr = lambda r: pl.BlockSpec((1, nh, nc, r, kt), lambda b, i: (b, 0, i, 0, 0))
        out_specs += [hm, tr(V_DIM + BF16_ROWS), tr(V_DIM), hm]
        out_shape += [jax.ShapeDtypeStruct((nb, nh, t, V_DIM), BF16),
                      jax.ShapeDtypeStruct((nb, nh, t // kt, V_DIM + BF16_ROWS, kt), BF16),
                      jax.ShapeDtypeStruct((nb, nh, t // kt, V_DIM, kt), BF16),
                      jax.ShapeDtypeStruct((nb, nh, t, V_DIM), BF16)]
    else:
        out_specs += [tok, tok]
        out_shape += [jax.ShapeDtypeStruct((nb, t, w), F32)] * 2
    return pl.pallas_call(
        functools.partial(_proj_kernel, prompt=prompt, kt=kt),
        grid=(nb, t // rows),
        in_specs=[
            pl.BlockSpec((1, rows, d), lambda b, i: (b, i, 0)),
            full((1, d)), full((1, d)), full((d, 2 * w)), full((d, 2 * w)),
            full((1, w)), full((1, w)),
            pl.BlockSpec((rows, LANE), lambda b, i: (i, 0)),
            pl.BlockSpec((rows, LANE), lambda b, i: (i, 0)),
            full((w, LANE)), full((2 * LANE, w)),
        ],
        out_specs=out_specs,
        out_shape=out_shape,
        compiler_params=pltpu.CompilerParams(
            dimension_semantics=("parallel", "parallel"), vmem_limit_bytes=VMEM_LIMIT_BYTES),
        name="proj_prompt" if prompt else "proj_sample",
    )(x, kv_norm.reshape(1, d), b_norm.reshape(1, d), kv_w.astype(BF16), b_w_in.astype(BF16),
      gk, gq, cos_t, sin_t, gather, expand)


def _diff_lambda(lq1, lk1, lq2, lk2, lam_init):
    return (jnp.exp(jnp.sum(lq1 * lk1, axis=-1, keepdims=True))
            - jnp.exp(jnp.sum(lq2 * lk2, axis=-1, keepdims=True)) + lam_init)


def _prompt_attn_kernel(qt_ref, khm_ref, vt_ref, sg_ref, x_ref, w_out_ref, hg_ref,
                        lq1_ref, lk1_ref, lq2_ref, lk2_ref, gk_ref, o_ref,
                        qm_s, shift_s, acc_s, p_s, oh_s, *, lam_init):
    nh, vd, tq = qt_ref.shape[1], qt_ref.shape[3], qt_ref.shape[4]
    qi = pl.program_id(1)
    chains = [(h, c) for h in range(nh) for c in range(2)]

    key_norm_bound = (math.sqrt(HEAD_DIM) * KEY_NORM_SLACK
                      * jnp.max(jnp.abs(gk_ref[...]), axis=-1, keepdims=True))
    feat = lax.broadcasted_iota(jnp.int32, (vd, tq), 0)
    for h in range(nh):
        qt = qt_ref[0, h, 0]
        zero = jnp.zeros_like(qt)
        for c in range(2):
            qm = jnp.where((feat >= HEAD_DIM) if c else (feat < HEAD_DIM), qt, zero)
            qm_s[h, c] = qm
            qf = qm.astype(F32)
            shift_s[h, c] = jnp.sqrt(jnp.sum(qf * qf, axis=0, keepdims=True)) * key_norm_bound

    def visible_mask():
        krow = lax.broadcasted_iota(jnp.int32, (tq, tq), 0)
        qcol = lax.broadcasted_iota(jnp.int32, (tq, tq), 1)
        return (krow // CHUNK) <= (qcol // CHUNK)

    def key_tile(h, kt):
        return khm_ref[0, h, pl.ds(pl.multiple_of(kt * tq, tq), tq), :]

    def tile_update(kt, masked):
        if masked:
            visible = visible_mask()
        for h, c in chains:
            s = jnp.dot(key_tile(h, kt), qm_s[h, c], preferred_element_type=F32)
            p = jnp.exp2(s - shift_s[h, c])
            if masked:
                p = jnp.where(visible, p, 0.0)
            p_s[h, c] = p.astype(BF16)
        for h, c in chains:
            acc_s[h, c] += jnp.dot(vt_ref[0, h, kt], p_s[h, c], preferred_element_type=F32)

    def fast_body(kt, carry):
        tile_update(kt, False)
        return carry

    acc_s[...] = jnp.zeros(acc_s.shape, F32)
    lax.fori_loop(0, qi, fast_body, 0)
    tile_update(qi, True)

    denom_min = jnp.min(acc_s[:, :, vd:vd + 1, :])

    @pl.when(jnp.logical_not(denom_min >= DENOM_FLOOR))
    def _():
        visible = visible_mask()

        def head_body(h, carry):
            def tile_step(kt, state, masked):
                new = []
                for c, (m, acc) in enumerate(state):
                    s = jnp.dot(key_tile(h, kt), qm_s[h, c], preferred_element_type=F32)
                    if masked:
                        s = jnp.where(visible, s, NEG_BIG)
                    m_new = jnp.maximum(m, jnp.max(s, axis=0, keepdims=True))
                    p = jnp.exp2(s - m_new).astype(BF16)
                    acc_new = jnp.exp2(m - m_new) * acc + jnp.dot(
                        vt_ref[0, h, kt], p, preferred_element_type=F32)
                    new.append((m_new, acc_new))
                return tuple(new)

            init = tuple((jnp.full((1, tq), NEG_BIG, F32), jnp.zeros(acc_s.shape[2:], F32))
                         for _ in range(2))
            state = lax.fori_loop(0, qi, lambda kt, st: tile_step(kt, st, False), init)
            state = tile_step(qi, state, True)
            for c in range(2):
                acc_s[h, c] = state[c][1]
            return carry

        lax.fori_loop(0, nh, head_body, 0)

    lam = _diff_lambda(lq1_ref[...], lk1_ref[...], lq2_ref[...], lk2_ref[...], lam_init)
    for h in range(nh):
        ot = (acc_s[h, 0, 0:vd, :] / acc_s[h, 0, vd:vd + 1, :]
              - lam * (acc_s[h, 1, 0:vd, :] / acc_s[h, 1, vd:vd + 1, :]))
        ms = jnp.mean(ot * ot, axis=0, keepdims=True)
        on = ot * lax.rsqrt(ms + EPS) * hg_ref[...] * (1.0 - lam_init)
        oh_s[:, h * vd:(h + 1) * vd] = (on.T * sg_ref[0, h].astype(F32)).astype(BF16)
    o_ref[0] = x_ref[0] + jnp.dot(oh_s[...], w_out_ref[...], preferred_element_type=F32)


def _prompt_attn(qt, khm, vt, sg, x, w_out, head_g, lq1, lk1, lq2, lk2, k_norm, *, lam_init):
    nb, nh, nq, vd, tq = qt.shape
    vde = vt.shape[3]
    t = khm.shape[2]
    d = x.shape[-1]
    w = nh * vd
    hg = jnp.broadcast_to(head_g.astype(F32)[:, None], (vd, tq))
    full = lambda shape: pl.BlockSpec(shape, lambda b, i: (0,) * len(shape))
    row = lambda a: a.reshape(1, -1).astype(F32)
    return pl.pallas_call(
        functools.partial(_prompt_attn_kernel, lam_init=lam_init),
        grid=(nb, nq),
        in_specs=[
            pl.BlockSpec((1, nh, 1, vd, tq), lambda b, i: (b, 0, i, 0, 0)),
            pl.BlockSpec((1, nh, t, vd), lambda b, i: (b, 0, 0, 0)),
            pl.BlockSpec((1, nh, nq, vde, tq), lambda b, i: (b, 0, 0, 0, 0)),
            pl.BlockSpec((1, nh, tq, vd), lambda b, i: (b, 0, i, 0)),
            pl.BlockSpec((1, tq, d), lambda b, i: (b, i, 0)),
            full((w, d)), full((vd, tq)),
            full((1, HEAD_DIM)), full((1, HEAD_DIM)), full((1, HEAD_DIM)), full((1, HEAD_DIM)),
            full((1, HEAD_DIM)),
        ],
        out_specs=pl.BlockSpec((1, tq, d), lambda b, i: (b, i, 0)),
        out_shape=jax.ShapeDtypeStruct((nb, t, d), F32),
        scratch_shapes=[
            pltpu.VMEM((nh, 2, vd, tq), BF16),
            pltpu.VMEM((nh, 2, 1, tq), F32),
            pltpu.VMEM((nh, 2, vde, tq), F32),
            pltpu.VMEM((nh, 2, tq, tq), BF16),
            pltpu.VMEM((tq, w), BF16),
        ],
        compiler_params=pltpu.CompilerParams(
            dimension_semantics=("parallel", "arbitrary"), vmem_limit_bytes=VMEM_LIMIT_BYTES),
        name="attn_prompt",
    )(qt, khm, vt, sg, x, w_out.astype(BF16), hg, row(lq1), row(lk1), row(lq2), row(lk2),
      row(k_norm))


def _sample_attn_kernel(q_ref, kn_ref, vn_ref, ck_ref, cv_ref, sg_ref, x_ref, w_out_ref, hg_ref,
                        lq1_ref, lk1_ref, lq2_ref, lk2_ref, o_ref, *, lam_init):
    q = q_ref[0]
    tq, w = q.shape
    nh = w // V_DIM
    lam = _diff_lambda(lq1_ref[...], lk1_ref[...], lq2_ref[...], lk2_ref[...], lam_init)
    lane = lax.broadcasted_iota(jnp.int32, (tq, V_DIM), 1)
    contract_last = (((1,), (1,)), ((), ()))
    outs = []
    for h in range(nh):
        cs = slice(h * V_DIM, (h + 1) * V_DIM)
        qh = q[:, cs]
        ck = ck_ref[0, :, cs].astype(BF16)
        cv = cv_ref[0, :, cs].astype(BF16)
        kn = kn_ref[0, :, cs].astype(BF16)
        vn = vn_ref[0, :, cs].astype(BF16)
        oh = None
        for c in range(2):
            in_map = (lane >= HEAD_DIM) if c else (lane < HEAD_DIM)
            q_c = jnp.where(in_map, qh, 0.0).astype(BF16)
            s_old = lax.dot_general(q_c, ck, contract_last, preferred_element_type=F32)
            s_new = lax.dot_general(q_c, kn, contract_last, preferred_element_type=F32)
            m = jnp.maximum(jnp.max(s_old, axis=-1, keepdims=True),
                            jnp.max(s_new, axis=-1, keepdims=True))
            p_old = jnp.exp2(s_old - m)
            p_new = jnp.exp2(s_new - m)
            l = jnp.sum(p_old, axis=-1, keepdims=True) + jnp.sum(p_new, axis=-1, keepdims=True)
            pv = (jnp.dot(p_old.astype(BF16), cv, preferred_element_type=F32)
                  + jnp.dot(p_new.astype(BF16), vn, preferred_element_type=F32)) / l
            oh = pv if c == 0 else oh - lam * pv
        ms = jnp.mean(oh * oh, axis=-1, keepdims=True)
        outs.append(oh * lax.rsqrt(ms + EPS) * hg_ref[...] * (1.0 - lam_init))
    o = (jnp.concatenate(outs, axis=1) * sg_ref[0]).astype(BF16)
    o_ref[0] = x_ref[0] + jnp.dot(o, w_out_ref[...], preferred_element_type=F32)


def _sample_attn(q, k_new, v_new, cache_k, cache_v, sg, x, w_out, head_g, lq1, lk1, lq2, lk2, *, lam_init):
    nb, tq, w = q.shape
    past = cache_k.shape[1]
    d = x.shape[-1]
    full = lambda shape: pl.BlockSpec(shape, lambda b: (0,) * len(shape))
    tok = lambda width: pl.BlockSpec((1, tq, width), lambda b: (b, 0, 0))
    old = pl.BlockSpec((1, past, w), lambda b: (b, 0, 0))
    row = lambda a: a.reshape(1, -1).astype(F32)
    return pl.pallas_call(
        functools.partial(_sample_attn_kernel, lam_init=lam_init),
        grid=(nb,),
        in_specs=[tok(w), tok(w), tok(w), old, old, tok(w), tok(d),
                  full((w, d)), full((1, V_DIM)),
                  full((1, HEAD_DIM)), full((1, HEAD_DIM)), full((1, HEAD_DIM)), full((1, HEAD_DIM))],
        out_specs=tok(d),
        out_shape=jax.ShapeDtypeStruct((nb, tq, d), F32),
        compiler_params=pltpu.CompilerParams(
            dimension_semantics=("parallel",), vmem_limit_bytes=VMEM_LIMIT_BYTES),
        name="attn_sample",
    )(q, k_new, v_new, cache_k.reshape(nb, past, w), cache_v.reshape(nb, past, w), sg, x,
      w_out.astype(BF16), row(head_g), row(lq1), row(lk1), row(lq2), row(lk2))


def _pick_tile(n, target):
    tile = min(n, target)
    while n % tile:
        tile //= 2
    return tile


def kernel(x_prompt, x_sample, state_conv, state_h, cache_k, cache_v, a_norm, a_w_in, a_conv_w, a_conv_b, a_gate_r_w, a_gate_r_b, a_gate_i_w, a_gate_i_b, a_lambda, a_w_out, kv_norm, kv_w, k_norm, b_norm, b_w_in, b_q_norm, b_lambda_q1, b_lambda_k1, b_lambda_q2, b_lambda_k2, b_head_norm, b_w_out):
    n_a, n_b = a_norm.shape[0], b_norm.shape[0]
    assert n_a == 1 and n_b == 1, "one recurrent layer followed by one attention layer"
    nb, t, d = x_prompt.shape
    nbs, ts, _ = x_sample.shape
    d_rnn = a_lambda.shape[-1]
    w = kv_w.shape[1] // 2
    nh = w // V_DIM
    past = cache_k.shape[1]
    halo = CONV_W - 1
    lam_init = 0.8 - 0.6 * math.exp(-0.3 * n_a)
    a_args = (a_norm[0], a_w_in[0], a_conv_w[0], a_conv_b[0], a_gate_r_w[0], a_gate_r_b[0],
              a_gate_i_w[0], a_gate_i_b[0], a_lambda[0], a_w_out[0])
    lam_args = (b_lambda_q1[0], b_lambda_k1[0], b_lambda_q2[0], b_lambda_k2[0])

    def conv_state_out(c, n):
        return jnp.swapaxes(c.reshape(halo, n, d_rnn), 0, 1)[None]

    x1, conv_p, h_p = _rnn_layer(
        x_prompt, jnp.zeros((halo * nb, d_rnn), F32), jnp.zeros((nb, d_rnn), F32), *a_args,
        reset=True, tt=_pick_tile(t, 64))
    tq = _pick_tile(t, 256)
    k_p, v_p, khm, vt, qt, sg = _proj(
        x1, jnp.arange(t, dtype=jnp.int32), kv_norm, b_norm[0], kv_w, b_w_in[0], k_norm, b_q_norm[0],
        prompt=True, rows=_pick_tile(t, 512), kt=tq)
    y_p = _prompt_attn(qt, khm, vt, sg, x1, b_w_out[0], b_head_norm[0], *lam_args, k_norm,
                       lam_init=lam_init)

    conv0 = jnp.swapaxes(state_conv[0], 0, 1).reshape(halo * nbs, d_rnn)
    x1s, conv_s, h_s = _rnn_layer(x_sample, conv0, state_h[0], *a_args, reset=False, tt=ts)
    k_s, v_s, q_s, sg_s = _proj(
        x1s, past + jnp.arange(ts, dtype=jnp.int32), kv_norm, b_norm[0], kv_w, b_w_in[0], k_norm,
        b_q_norm[0], prompt=False, rows=ts, kt=ts)
    y_s = _sample_attn(q_s, k_s, v_s, cache_k, cache_v, sg_s, x1s, b_w_out[0], b_head_norm[0],
                       *lam_args, lam_init=lam_init)

    return (y_p, y_s,
            conv_state_out(conv_p, nb), h_p[None],
            k_p.reshape(nb, t, nh, V_DIM), v_p.reshape(nb, t, nh, V_DIM),
            conv_state_out(conv_s, nbs), h_s[None],
            k_s.reshape(nbs, ts, nh, V_DIM), v_s.reshape(nbs, ts, nh, V_DIM))
```

```python
import functools
import math

import numpy as np
import jax
import jax.numpy as jnp
from jax import lax
from jax.experimental import pallas as pl
from jax.experimental.pallas import tpu as pltpu

LANE = 128
SUBLANE = 8
VMEM_LIMIT_BYTES = 56 * 1024 * 1024

CHUNK = 64
CONV_W = 4
LRU_C = 8.0
HEAD_DIM = 64
V_DIM = 2 * HEAD_DIM
ROPE_THETA = 10000.0
EPS = 1e-6
LOG2E = 1.4426950408889634
NEG_BIG = -1e30
BF16_ROWS = 16
KEY_NORM_SLACK = 1.0 + 2.0 ** -6
DENOM_FLOOR = 2.0 ** -100

F32 = jnp.float32
BF16 = jnp.bfloat16


def _gate_band_layout(d_rnn, blk):
    ncol = d_rnn // LANE
    spans = []
    for j in range(ncol):
        lo = ((LANE * j) // blk) * blk
        hi = ((LANE * j + LANE - 1) // blk + 1) * blk
        spans.append((lo, hi))
    width = max(-(-(hi - (lo // LANE) * LANE) // LANE) * LANE for lo, hi in spans)
    starts = []
    for lo, hi in spans:
        ws = min((lo // LANE) * LANE, d_rnn - width)
        assert ws <= lo and hi <= ws + width
        starts.append(ws)
    return tuple(starts), width


def _block_diag_dense(w):
    n, blk, _ = w.shape
    eye = jnp.eye(n, dtype=w.dtype)
    return jnp.einsum('nij,nm->nimj', w, eye).reshape(n * blk, n * blk)


def _rnn_layer_kernel(x_ref, conv0_ref, h0_ref, g_ref, w_in_ref, cw_ref, cb_ref, wg_ref, bg_ref,
                      lam_ref, w_out_ref, o_ref, conv_o_ref, h_o_ref,
                      ubuf, a_s, b_s, y_s, hc, *, reset, starts, gate_k):
    nb, tt, d = x_ref.shape
    rows = nb * tt
    d_rnn = lam_ref.shape[1]
    halo = (CONV_W - 1) * nb
    step = pl.program_id(0)

    @pl.when(step == 0)
    def _():
        ubuf[0:halo, :] = conv0_ref[...]
        hc[...] = h0_ref[...]

    x = jnp.swapaxes(x_ref[...], 0, 1).reshape(rows, d)
    ms = jnp.mean(x * x, axis=-1, keepdims=True)
    xn = (x * lax.rsqrt(ms + EPS) * g_ref[...]).astype(BF16)
    h = jnp.dot(xn, w_in_ref[...], preferred_element_type=F32)
    ubuf[halo:halo + rows, :] = h[:, :d_rnn]
    half_gate = h[:, d_rnn:]

    u = cb_ref[...] + cw_ref[CONV_W - 1:CONV_W, :] * ubuf[halo:halo + rows, :]
    for j in range(CONV_W - 1):
        u = u + cw_ref[j:j + 1, :] * ubuf[j * nb:j * nb + rows, :]
    ubuf[0:halo, :] = ubuf[rows:rows + halo, :]
    conv_o_ref[...] = ubuf[0:halo, :]

    lam = lam_ref[...]
    softplus_neg = jnp.maximum(-lam, 0.0) + jnp.log1p(jnp.exp(-jnp.abs(lam)))
    c2 = (-0.5 * LOG2E * LRU_C) * softplus_neg

    u_bf = u.astype(BF16)
    for j, ws in enumerate(starts):
        cs = slice(j * LANE, (j + 1) * LANE)
        ri = jnp.dot(u_bf[:, ws:ws + gate_k], wg_ref[j], preferred_element_type=F32) + bg_ref[j]
        t_r = jnp.tanh(ri[:, :LANE])
        t_i = jnp.tanh(ri[:, LANE:])
        a = jnp.exp2(c2[:, cs] * t_r + c2[:, cs])
        y = 1.0 - a * a
        mult = jnp.where(y > 0.0, y * lax.rsqrt(y), 0.0)
        if reset:
            head = jnp.where(step == 0, 1.0, mult[0:nb])
            mult = jnp.concatenate([head, mult[nb:]], axis=0)
        a_s[:, cs] = a
        b_s[:, cs] = (mult * u[:, cs]) * (0.5 * t_i + 0.5)

    def scan_step(t, hcur):
        r0 = pl.multiple_of(t * nb, SUBLANE)
        hn = a_s[pl.ds(r0, nb), :] * hcur + b_s[pl.ds(r0, nb), :]
        y_s[pl.ds(r0, nb), :] = hn
        return hn

    hlast = lax.fori_loop(0, tt, scan_step, hc[...], unroll=min(tt, 8))
    hc[...] = hlast
    h_o_ref[...] = hlast

    yg = (y_s[...] * (half_gate * (jnp.tanh(half_gate) + 1.0))).astype(BF16)
    out = x + jnp.dot(yg, w_out_ref[...], preferred_element_type=F32)
    o_ref[...] = jnp.swapaxes(out.reshape(tt, nb, d), 0, 1)


def _rnn_layer(x, conv0, h0, norm_g, w_in, conv_w, conv_b, wr, br, wi, bi, lam, w_out, *, reset, tt):
    nb, t, d = x.shape
    assert nb == SUBLANE and t % tt == 0 and tt % SUBLANE == 0
    d_rnn = lam.shape[-1]
    n_blk, blk, _ = wr.shape
    starts, gate_k = _gate_band_layout(d_rnn, blk)
    ncol = d_rnn // LANE
    rows = nb * tt
    halo = (CONV_W - 1) * nb

    dense_r = _block_diag_dense(wr)
    dense_i = _block_diag_dense(wi)
    wg = jnp.stack([
        jnp.concatenate([dense_r[ws:ws + gate_k, j * LANE:(j + 1) * LANE],
                         dense_i[ws:ws + gate_k, j * LANE:(j + 1) * LANE]], axis=1)
        for j, ws in enumerate(starts)])
    bg = jnp.concatenate([br.reshape(ncol, 1, LANE), bi.reshape(ncol, 1, LANE)], axis=2)
    wg = (0.5 * wg).astype(BF16)
    bg = 0.5 * bg
    gate_cols = jnp.arange(2 * d_rnn) >= d_rnn
    w_in = (w_in * jnp.where(gate_cols, 0.5, 1.0)).astype(BF16)

    full = lambda shape: pl.BlockSpec(shape, lambda i: (0,) * len(shape))
    kern = functools.partial(_rnn_layer_kernel, reset=reset, starts=starts, gate_k=gate_k)
    return pl.pallas_call(
        kern,
        grid=(t // tt,),
        in_specs=[
            pl.BlockSpec((nb, tt, d), lambda i: (0, i, 0)),
            full((halo, d_rnn)), full((nb, d_rnn)), full((1, d)),
            full((d, 2 * d_rnn)), full((CONV_W, d_rnn)), full((1, d_rnn)),
            full((ncol, gate_k, 2 * LANE)), full((ncol, 1, 2 * LANE)), full((1, d_rnn)),
            full((d_rnn, d)),
        ],
        out_specs=[
            pl.BlockSpec((nb, tt, d), lambda i: (0, i, 0)),
            full((halo, d_rnn)), full((nb, d_rnn)),
        ],
        out_shape=[
            jax.ShapeDtypeStruct((nb, t, d), F32),
            jax.ShapeDtypeStruct((halo, d_rnn), F32),
            jax.ShapeDtypeStruct((nb, d_rnn), F32),
        ],
        scratch_shapes=[
            pltpu.VMEM((halo + rows, d_rnn), F32),
            pltpu.VMEM((rows, d_rnn), F32),
            pltpu.VMEM((rows, d_rnn), F32),
            pltpu.VMEM((rows, d_rnn), F32),
            pltpu.VMEM((nb, d_rnn), F32),
        ],
        compiler_params=pltpu.CompilerParams(
            dimension_semantics=("arbitrary",), vmem_limit_bytes=VMEM_LIMIT_BYTES),
        name="rnn_layer_reset" if reset else "rnn_layer_state",
    )(x, conv0, h0, norm_g.reshape(1, d), w_in, conv_w, conv_b.reshape(1, d_rnn),
      wg, bg, lam.reshape(1, d_rnn), w_out.astype(BF16))


def _group_sum_matrices(width):
    groups = width // HEAD_DIM
    assert groups <= LANE
    gather = np.zeros((width, LANE), np.float32)
    gather[np.arange(width), np.arange(width) // HEAD_DIM] = 1.0
    expand = np.concatenate([gather.T, gather.T], axis=0)
    return jnp.asarray(gather, BF16), jnp.asarray(expand, BF16)


def _group_rms_scale(x, gather, expand):
    ssum = jnp.dot((x * x).astype(BF16), gather, preferred_element_type=F32)
    inv = lax.rsqrt(ssum * (1.0 / HEAD_DIM) + EPS)
    hi = inv.astype(BF16)
    lo = (inv - hi.astype(F32)).astype(BF16)
    return jnp.dot(jnp.concatenate([hi, lo], axis=1), expand, preferred_element_type=F32)


def _rope_slab(xh, cos_t, sin_t):
    lane = lax.broadcasted_iota(jnp.int32, xh.shape, 1)
    first_half = (lane % HEAD_DIM) < (HEAD_DIM // 2)
    partner = jnp.where(first_half, pltpu.roll(xh, LANE - HEAD_DIM // 2, 1),
                        pltpu.roll(xh, HEAD_DIM // 2, 1))
    return xh * cos_t + partner * sin_t


def _proj_kernel(x_ref, gkv_ref, gb_ref, kvw_ref, bw_ref, gk_ref, gq_ref, cos_ref, sin_ref,
                 gather_ref, expand_ref, *out_refs, prompt, kt):
    if prompt:
        k_ref, v_ref, khm_ref, vt_ref, qt_ref, sg_ref = out_refs
    else:
        k_ref, v_ref, q_ref, sg_ref = out_refs
    x = x_ref[0]
    rows, d = x.shape
    w = gk_ref.shape[1]
    nh = w // V_DIM
    ms = jnp.mean(x * x, axis=-1, keepdims=True)
    xh = x * lax.rsqrt(ms + EPS)
    kv = jnp.dot((xh * gkv_ref[...]).astype(BF16), kvw_ref[...], preferred_element_type=F32)
    qg = jnp.dot((xh * gb_ref[...]).astype(BF16), bw_ref[...], preferred_element_type=F32)
    k_raw, v = kv[:, :w], kv[:, w:]
    q_raw, half_gate = qg[:, :w], qg[:, w:]
    gather, expand = gather_ref[...], expand_ref[...]
    kn = k_raw * _group_rms_scale(k_raw, gather, expand) * gk_ref[...]
    qn = q_raw * _group_rms_scale(q_raw, gather, expand) * gq_ref[...]
    sg = half_gate * (jnp.tanh(half_gate) + 1.0)
    cos_t, sin_t = cos_ref[...], sin_ref[...]
    if prompt:
        pad_row = lax.broadcasted_iota(jnp.int32, (BF16_ROWS, kt), 0)
        ones_row = jnp.where(pad_row == 0, 1.0, 0.0).astype(BF16)
    else:
        sg_ref[0] = sg
    for h in range(nh):
        cs = slice(h * V_DIM, (h + 1) * V_DIM)
        kh = _rope_slab(kn[:, cs], cos_t, sin_t)
        qh = _rope_slab(qn[:, cs], cos_t, sin_t)
        k_ref[0, pl.ds(h, rows, stride=nh), :] = kh
        v_ref[0, pl.ds(h, rows, stride=nh), :] = v[:, cs]
        if prompt:
            khm_ref[0, h] = kh.astype(BF16)
            sg_ref[0, h] = sg[:, cs].astype(BF16)
            for c in range(rows // kt):
                rs = slice(c * kt, (c + 1) * kt)
                vt_ref[0, h, c, 0:V_DIM, :] = v[rs, cs].T.astype(BF16)
                vt_ref[0, h, c, V_DIM:V_DIM + BF16_ROWS, :] = ones_row
                qt_ref[0, h, c] = qh[rs, :].T.astype(BF16)
        else:
            q_ref[0, :, cs] = qh


def _rope_tables(pos):
    half = HEAD_DIM // 2
    inv = ROPE_THETA ** (-jnp.arange(half, dtype=F32) / half)
    ang = pos.astype(F32)[:, None] * inv[None, :]
    cos, sin = jnp.cos(ang), jnp.sin(ang)
    reps = LANE // HEAD_DIM
    return (jnp.tile(jnp.concatenate([cos, cos], axis=1), (1, reps)),
            jnp.tile(jnp.concatenate([-sin, sin], axis=1), (1, reps)))


def _proj(x, pos, kv_norm, b_norm, kv_w, b_w_in, k_norm, q_norm, *, prompt, rows, kt):
    nb, t, d = x.shape
    w = kv_w.shape[1] // 2
    nh = w // V_DIM
    assert t % rows == 0 and (not prompt or rows % kt == 0)
    cos_t, sin_t = _rope_tables(pos)
    gather, expand = _group_sum_matrices(w)
    reps = w // HEAD_DIM
    gk = jnp.tile(k_norm.astype(F32), reps).reshape(1, w)
    gq = (jnp.tile(q_norm.astype(F32), reps) * (HEAD_DIM ** -0.5 * LOG2E)).reshape(1, w)
    b_w_half = (b_w_in * jnp.where(jnp.arange(2 * w) >= w, 0.5, 1.0)).astype(BF16)

    full = lambda shape: pl.BlockSpec(shape, lambda b, i: (0,) * len(shape))
    tok = pl.BlockSpec((1, rows, w), lambda b, i: (b, i, 0))
    tok_heads = pl.BlockSpec((1, rows * nh, V_DIM), lambda b, i: (b, i, 0))
    out_specs = [tok_heads, tok_heads]
    out_shape = [jax.ShapeDtypeStruct((nb, t * nh, V_DIM), F32)] * 2
    if prompt:
        nc = rows // kt
        hm = pl.BlockSpec((1, nh, rows, V_DIM), lambda b, i: (b, 0, i, 0))
        tr = lambda r: pl.BlockSpec((1, nh, nc, r, kt), lambda b, i: (b, 0, i, 0, 0))
        out_specs += [hm, tr(V_DIM + BF16_ROWS), tr(V_DIM), hm]
        out_shape += [jax.ShapeDtypeStruct((nb, nh, t, V_DIM), BF16),
                      jax.ShapeDtypeStruct((nb, nh, t // kt, V_DIM + BF16_ROWS, kt), BF16),
                      jax.ShapeDtypeStruct((nb, nh, t // kt, V_DIM, kt), BF16),
                      jax.ShapeDtypeStruct((nb, nh, t, V_DIM), BF16)]
    else:
        out_specs += [tok, tok]
        out_shape += [jax.ShapeDtypeStruct((nb, t, w), F32)] * 2
    return pl.pallas_call(
        functools.partial(_proj_kernel, prompt=prompt, kt=kt),
        grid=(nb, t // rows),
        in_specs=[
            pl.BlockSpec((1, rows, d), lambda b, i: (b, i, 0)),
            full((1, d)), full((1, d)), full((d, 2 * w)), full((d, 2 * w)),
            full((1, w)), full((1, w)),
            pl.BlockSpec((rows, LANE), lambda b, i: (i, 0)),
            pl.BlockSpec((rows, LANE), lambda b, i: (i, 0)),
            full((w, LANE)), full((2 * LANE, w)),
        ],
        out_specs=out_specs,
        out_shape=out_shape,
        compiler_params=pltpu.CompilerParams(
            dimension_semantics=("parallel", "parallel"), vmem_limit_bytes=VMEM_LIMIT_BYTES),
        name="proj_prompt" if prompt else "proj_sample",
    )(x, kv_norm.reshape(1, d), b_norm.reshape(1, d), kv_w.astype(BF16), b_w_half,
      gk, gq, cos_t, sin_t, gather, expand)


def _diff_lambda(lq1, lk1, lq2, lk2, lam_init):
    return (jnp.exp(jnp.sum(lq1 * lk1, axis=-1, keepdims=True))
            - jnp.exp(jnp.sum(lq2 * lk2, axis=-1, keepdims=True)) + lam_init)


def _prompt_attn_kernel(qt_ref, khm_ref, vt_ref, sg_ref, x_ref, w_out_ref, hg_ref,
                        lq1_ref, lk1_ref, lq2_ref, lk2_ref, gk_ref, o_ref,
                        qm_s, shift_s, acc_s, p_s, oh_s, *, lam_init):
    nh, vd, tq = qt_ref.shape[1], qt_ref.shape[3], qt_ref.shape[4]
    qi = pl.program_id(1)
    chains = [(h, c) for h in range(nh) for c in range(2)]

    key_norm_bound = (math.sqrt(HEAD_DIM) * KEY_NORM_SLACK
                      * jnp.max(jnp.abs(gk_ref[...]), axis=-1, keepdims=True))
    feat = lax.broadcasted_iota(jnp.int32, (vd, tq), 0)
    for h in range(nh):
        qt = qt_ref[0, h, 0]
        zero = jnp.zeros_like(qt)
        for c in range(2):
            qm = jnp.where((feat >= HEAD_DIM) if c else (feat < HEAD_DIM), qt, zero)
            qm_s[h, c] = qm
            qf = qm.astype(F32)
            shift_s[h, c] = jnp.sqrt(jnp.sum(qf * qf, axis=0, keepdims=True)) * key_norm_bound

    def visible_mask():
        krow = lax.broadcasted_iota(jnp.int32, (tq, tq), 0)
        qcol = lax.broadcasted_iota(jnp.int32, (tq, tq), 1)
        return (krow // CHUNK) <= (qcol // CHUNK)

    def key_tile(h, kt):
        return khm_ref[0, h, pl.ds(pl.multiple_of(kt * tq, tq), tq), :]

    def tile_update(kt, masked):
        if masked:
            visible = visible_mask()
        for h, c in chains:
            s = jnp.dot(key_tile(h, kt), qm_s[h, c], preferred_element_type=F32)
            p = jnp.exp2(s - shift_s[h, c])
            if masked:
                p = jnp.where(visible, p, 0.0)
            p_s[h, c] = p.astype(BF16)
        for h, c in chains:
            acc_s[h, c] += jnp.dot(vt_ref[0, h, kt], p_s[h, c], preferred_element_type=F32)

    def fast_body(kt, carry):
        tile_update(kt, False)
        return carry

    acc_s[...] = jnp.zeros(acc_s.shape, F32)
    lax.fori_loop(0, qi, fast_body, 0)
    tile_update(qi, True)

    denom_min = jnp.min(acc_s[:, :, vd:vd + 1, :])

    @pl.when(jnp.logical_not(denom_min >= DENOM_FLOOR))
    def _():
        visible = visible_mask()

        def head_body(h, carry):
            def tile_step(kt, state, masked):
                new = []
                for c, (m, acc) in enumerate(state):
                    s = jnp.dot(key_tile(h, kt), qm_s[h, c], preferred_element_type=F32)
                    if masked:
                        s = jnp.where(visible, s, NEG_BIG)
                    m_new = jnp.maximum(m, jnp.max(s, axis=0, keepdims=True))
                    p = jnp.exp2(s - m_new).astype(BF16)
                    acc_new = jnp.exp2(m - m_new) * acc + jnp.dot(
                        vt_ref[0, h, kt], p, preferred_element_type=F32)
                    new.append((m_new, acc_new))
                return tuple(new)

            init = tuple((jnp.full((1, tq), NEG_BIG, F32), jnp.zeros(acc_s.shape[2:], F32))
                         for _ in range(2))
            state = lax.fori_loop(0, qi, lambda kt, st: tile_step(kt, st, False), init)
            state = tile_step(qi, state, True)
            for c in range(2):
                acc_s[h, c] = state[c][1]
            return carry

        lax.fori_loop(0, nh, head_body, 0)

    lam = _diff_lambda(lq1_ref[...], lk1_ref[...], lq2_ref[...], lk2_ref[...], lam_init)
    for h in range(nh):
        w1 = 1.0 / acc_s[h, 0, vd:vd + 1, :]
        w2 = lam / acc_s[h, 1, vd:vd + 1, :]
        ot = acc_s[h, 0, 0:vd, :] * w1 - acc_s[h, 1, 0:vd, :] * w2
        ms = jnp.mean(ot * ot, axis=0, keepdims=True)
        on = (ot * lax.rsqrt(ms + EPS)) * hg_ref[...]
        oh_s[:, h * vd:(h + 1) * vd] = (on.T * sg_ref[0, h].astype(F32)).astype(BF16)
    o_ref[0] = x_ref[0] + jnp.dot(oh_s[...], w_out_ref[...], preferred_element_type=F32)


def _prompt_attn(qt, khm, vt, sg, x, w_out, head_g, lq1, lk1, lq2, lk2, k_norm, *, lam_init):
    nb, nh, nq, vd, tq = qt.shape
    vde = vt.shape[3]
    t = khm.shape[2]
    d = x.shape[-1]
    w = nh * vd
    hg = jnp.broadcast_to((head_g.astype(F32) * (1.0 - lam_init))[:, None], (vd, tq))
    full = lambda shape: pl.BlockSpec(shape, lambda b, i: (0,) * len(shape))
    row = lambda a: a.reshape(1, -1).astype(F32)
    return pl.pallas_call(
        functools.partial(_prompt_attn_kernel, lam_init=lam_init),
        grid=(nb, nq),
        in_specs=[
            pl.BlockSpec((1, nh, 1, vd, tq), lambda b, i: (b, 0, i, 0, 0)),
            pl.BlockSpec((1, nh, t, vd), lambda b, i: (b, 0, 0, 0)),
            pl.BlockSpec((1, nh, nq, vde, tq), lambda b, i: (b, 0, 0, 0, 0)),
            pl.BlockSpec((1, nh, tq, vd), lambda b, i: (b, 0, i, 0)),
            pl.BlockSpec((1, tq, d), lambda b, i: (b, i, 0)),
            full((w, d)), full((vd, tq)),
            full((1, HEAD_DIM)), full((1, HEAD_DIM)), full((1, HEAD_DIM)), full((1, HEAD_DIM)),
            full((1, HEAD_DIM)),
        ],
        out_specs=pl.BlockSpec((1, tq, d), lambda b, i: (b, i, 0)),
        out_shape=jax.ShapeDtypeStruct((nb, t, d), F32),
        scratch_shapes=[
            pltpu.VMEM((nh, 2, vd, tq), BF16),
            pltpu.VMEM((nh, 2, 1, tq), F32),
            pltpu.VMEM((nh, 2, vde, tq), F32),
            pltpu.VMEM((nh, 2, tq, tq), BF16),
            pltpu.VMEM((tq, w), BF16),
        ],
        compiler_params=pltpu.CompilerParams(
            dimension_semantics=("parallel", "arbitrary"), vmem_limit_bytes=VMEM_LIMIT_BYTES),
        name="attn_prompt",
    )(qt, khm, vt, sg, x, w_out.astype(BF16), hg, row(lq1), row(lk1), row(lq2), row(lk2),
      row(k_norm))


def _sample_attn_kernel(q_ref, kn_ref, vn_ref, ck_ref, cv_ref, sg_ref, x_ref, w_out_ref, hg_ref,
                        lq1_ref, lk1_ref, lq2_ref, lk2_ref, o_ref, *, lam_init):
    q = q_ref[0]
    tq, w = q.shape
    nh = w // V_DIM
    lam = _diff_lambda(lq1_ref[...], lk1_ref[...], lq2_ref[...], lk2_ref[...], lam_init)
    lane = lax.broadcasted_iota(jnp.int32, (tq, V_DIM), 1)
    contract_last = (((1,), (1,)), ((), ()))
    outs = []
    for h in range(nh):
        cs = slice(h * V_DIM, (h + 1) * V_DIM)
        qh = q[:, cs]
        ck = ck_ref[0, :, cs].astype(BF16)
        cv = cv_ref[0, :, cs].astype(BF16)
        kn = kn_ref[0, pl.ds(h, tq, stride=nh), :].astype(BF16)
        vn = vn_ref[0, pl.ds(h, tq, stride=nh), :].astype(BF16)
        q2 = jnp.concatenate([jnp.where(lane < HEAD_DIM, qh, 0.0),
                              jnp.where(lane >= HEAD_DIM, qh, 0.0)], axis=0).astype(BF16)
        s_old = lax.dot_general(q2, ck, contract_last, preferred_element_type=F32)
        s_new = lax.dot_general(q2, kn, contract_last, preferred_element_type=F32)
        m = jnp.maximum(jnp.max(s_old, axis=-1, keepdims=True),
                        jnp.max(s_new, axis=-1, keepdims=True))
        p_old = jnp.exp2(s_old - m)
        p_new = jnp.exp2(s_new - m)
        l = jnp.sum(p_old, axis=-1, keepdims=True) + jnp.sum(p_new, axis=-1, keepdims=True)
        pv = (jnp.dot(p_old.astype(BF16), cv, preferred_element_type=F32)
              + jnp.dot(p_new.astype(BF16), vn, preferred_element_type=F32)) / l
        oh = pv[0:tq] - lam * pv[tq:2 * tq]
        ms = jnp.mean(oh * oh, axis=-1, keepdims=True)
        outs.append(oh * lax.rsqrt(ms + EPS) * hg_ref[...] * (1.0 - lam_init))
    o = (jnp.concatenate(outs, axis=1) * sg_ref[0]).astype(BF16)
    o_ref[0] = x_ref[0] + jnp.dot(o, w_out_ref[...], preferred_element_type=F32)


def _sample_attn(q, k_new, v_new, cache_k, cache_v, sg, x, w_out, head_g, lq1, lk1, lq2, lk2, *, lam_init):
    nb, tq, w = q.shape
    past = cache_k.shape[1]
    d = x.shape[-1]
    full = lambda shape: pl.BlockSpec(shape, lambda b: (0,) * len(shape))
    tok = lambda width: pl.BlockSpec((1, tq, width), lambda b: (b, 0, 0))
    old = pl.BlockSpec((1, past, w), lambda b: (b, 0, 0))
    new = pl.BlockSpec((1, tq * (w // V_DIM), V_DIM), lambda b: (b, 0, 0))
    row = lambda a: a.reshape(1, -1).astype(F32)
    return pl.pallas_call(
        functools.partial(_sample_attn_kernel, lam_init=lam_init),
        grid=(nb,),
        in_specs=[tok(w), new, new, old, old, tok(w), tok(d),
                  full((w, d)), full((1, V_DIM)),
                  full((1, HEAD_DIM)), full((1, HEAD_DIM)), full((1, HEAD_DIM)), full((1, HEAD_DIM))],
        out_specs=tok(d),
        out_shape=jax.ShapeDtypeStruct((nb, tq, d), F32),
        compiler_params=pltpu.CompilerParams(
            dimension_semantics=("parallel",), vmem_limit_bytes=VMEM_LIMIT_BYTES),
        name="attn_sample",
    )(q, k_new, v_new, cache_k.reshape(nb, past, w), cache_v.reshape(nb, past, w), sg, x,
      w_out.astype(BF16), row(head_g), row(lq1), row(lk1), row(lq2), row(lk2))


def _pick_tile(n, target):
    tile = min(n, target)
    while n % tile:
        tile //= 2
    return tile


def kernel(x_prompt, x_sample, state_conv, state_h, cache_k, cache_v, a_norm, a_w_in, a_conv_w, a_conv_b, a_gate_r_w, a_gate_r_b, a_gate_i_w, a_gate_i_b, a_lambda, a_w_out, kv_norm, kv_w, k_norm, b_norm, b_w_in, b_q_norm, b_lambda_q1, b_lambda_k1, b_lambda_q2, b_lambda_k2, b_head_norm, b_w_out):
    n_a, n_b = a_norm.shape[0], b_norm.shape[0]
    assert n_a == 1 and n_b == 1, "one recurrent layer followed by one attention layer"
    nb, t, d = x_prompt.shape
    nbs, ts, _ = x_sample.shape
    d_rnn = a_lambda.shape[-1]
    w = kv_w.shape[1] // 2
    nh = w // V_DIM
    past = cache_k.shape[1]
    halo = CONV_W - 1
    lam_init = 0.8 - 0.6 * math.exp(-0.3 * n_a)
    a_args = (a_norm[0], a_w_in[0], a_conv_w[0], a_conv_b[0], a_gate_r_w[0], a_gate_r_b[0],
              a_gate_i_w[0], a_gate_i_b[0], a_lambda[0], a_w_out[0])
    lam_args = (b_lambda_q1[0], b_lambda_k1[0], b_lambda_q2[0], b_lambda_k2[0])

    def conv_state_out(c, n):
        return jnp.swapaxes(c.reshape(halo, n, d_rnn), 0, 1)[None]

    x1, conv_p, h_p = _rnn_layer(
        x_prompt, jnp.zeros((halo * nb, d_rnn), F32), jnp.zeros((nb, d_rnn), F32), *a_args,
        reset=True, tt=_pick_tile(t, 64))
    tq = _pick_tile(t, 256)
    k_p, v_p, khm, vt, qt, sg = _proj(
        x1, jnp.arange(t, dtype=jnp.int32), kv_norm, b_norm[0], kv_w, b_w_in[0], k_norm, b_q_norm[0],
        prompt=True, rows=_pick_tile(t, 512), kt=tq)
    y_p = _prompt_attn(qt, khm, vt, sg, x1, b_w_out[0], b_head_norm[0], *lam_args, k_norm,
                       lam_init=lam_init)

    conv0 = jnp.swapaxes(state_conv[0], 0, 1).reshape(halo * nbs, d_rnn)
    x1s, conv_s, h_s = _rnn_layer(x_sample, conv0, state_h[0], *a_args, reset=False, tt=ts)
    k_s, v_s, q_s, sg_s = _proj(
        x1s, past + jnp.arange(ts, dtype=jnp.int32), kv_norm, b_norm[0], kv_w, b_w_in[0], k_norm,
        b_q_norm[0], prompt=False, rows=ts, kt=ts)
    y_s = _sample_attn(q_s, k_s, v_s, cache_k, cache_v, sg_s, x1s, b_w_out[0], b_head_norm[0],
                       *lam_args, lam_init=lam_init)

    return (y_p, y_s,
            conv_state_out(conv_p, nb), h_p[None],
            k_p.reshape(nb, t, nh, V_DIM), v_p.reshape(nb, t, nh, V_DIM),
            conv_state_out(conv_s, nbs), h_s[None],
            k_s.reshape(nbs, ts, nh, V_DIM), v_s.reshape(nbs, ts, nh, V_DIM))
```

```python
import functools
import math

import numpy as np
import jax
import jax.numpy as jnp
from jax import lax
from jax.experimental import pallas as pl
from jax.experimental.pallas import tpu as pltpu

LANE = 128
SUBLANE = 8
VMEM_LIMIT_BYTES = 56 * 1024 * 1024

CHUNK = 64
CONV_W = 4
LRU_C = 8.0
HEAD_DIM = 64
V_DIM = 2 * HEAD_DIM
ROPE_THETA = 10000.0
EPS = 1e-6
LOG2E = 1.4426950408889634
NEG_BIG = -1e30
BF16_ROWS = 16
KEY_NORM_SLACK = 1.0 + 2.0 ** -6
DENOM_FLOOR = 2.0 ** -100

F32 = jnp.float32
BF16 = jnp.bfloat16


def _gate_band_layout(d_rnn, blk):
    ncol = d_rnn // LANE
    spans = []
    for j in range(ncol):
        lo = ((LANE * j) // blk) * blk
        hi = ((LANE * j + LANE - 1) // blk + 1) * blk
        spans.append((lo, hi))
    width = max(-(-(hi - (lo // LANE) * LANE) // LANE) * LANE for lo, hi in spans)
    starts = []
    for lo, hi in spans:
        ws = min((lo // LANE) * LANE, d_rnn - width)
        assert ws <= lo and hi <= ws + width
        starts.append(ws)
    return tuple(starts), width


def _block_diag_dense(w):
    n, blk, _ = w.shape
    return jnp.concatenate(
        [jnp.pad(w[i], ((0, 0), (blk * i, blk * (n - 1 - i)))) for i in range(n)], axis=0)


def _rnn_layer_kernel(x_ref, conv0_ref, h0_ref, g_ref, w_in_ref, cw_ref, cb_ref, wg_ref, bg_ref,
                      lam_ref, w_out_ref, o_ref, conv_o_ref, h_o_ref,
                      ubuf, a_s, b_s, y_s, hc, *, reset, starts, gate_k):
    nb, tt, d = x_ref.shape
    rows = nb * tt
    d_rnn = lam_ref.shape[1]
    halo = (CONV_W - 1) * nb
    step = pl.program_id(0)

    @pl.when(step == 0)
    def _():
        ubuf[0:halo, :] = conv0_ref[...]
        hc[...] = h0_ref[...]

    x = jnp.swapaxes(x_ref[...], 0, 1).reshape(rows, d)
    ms = jnp.mean(x * x, axis=-1, keepdims=True)
    xn = (x * lax.rsqrt(ms + EPS) * g_ref[...]).astype(BF16)
    h = jnp.dot(xn, w_in_ref[...], preferred_element_type=F32)
    ubuf[halo:halo + rows, :] = h[:, :d_rnn]
    half_gate = h[:, d_rnn:]

    u = cb_ref[...] + cw_ref[CONV_W - 1:CONV_W, :] * ubuf[halo:halo + rows, :]
    for j in range(CONV_W - 1):
        u = u + cw_ref[j:j + 1, :] * ubuf[j * nb:j * nb + rows, :]
    ubuf[0:halo, :] = ubuf[rows:rows + halo, :]
    conv_o_ref[...] = ubuf[0:halo, :]

    lam = lam_ref[...]
    softplus_neg = jnp.maximum(-lam, 0.0) + jnp.log1p(jnp.exp(-jnp.abs(lam)))
    c2 = (-0.5 * LOG2E * LRU_C) * softplus_neg

    u_bf = u.astype(BF16)
    for j, ws in enumerate(starts):
        cs = slice(j * LANE, (j + 1) * LANE)
        ri = jnp.dot(u_bf[:, ws:ws + gate_k], wg_ref[j], preferred_element_type=F32) + bg_ref[j]
        t_r = jnp.tanh(ri[:, :LANE])
        t_i = jnp.tanh(ri[:, LANE:])
        a = jnp.exp2(c2[:, cs] * t_r + c2[:, cs])
        y = 1.0 - a * a
        mult = jnp.where(y > 0.0, y * lax.rsqrt(y), 0.0)
        if reset:
            head = jnp.where(step == 0, 1.0, mult[0:nb])
            mult = jnp.concatenate([head, mult[nb:]], axis=0)
        a_s[:, cs] = a
        b_s[:, cs] = (mult * u[:, cs]) * (0.5 * t_i + 0.5)

    def scan_step(t, hcur):
        r0 = pl.multiple_of(t * nb, SUBLANE)
        hn = a_s[pl.ds(r0, nb), :] * hcur + b_s[pl.ds(r0, nb), :]
        y_s[pl.ds(r0, nb), :] = hn
        return hn

    hlast = lax.fori_loop(0, tt, scan_step, hc[...], unroll=min(tt, 8))
    hc[...] = hlast
    h_o_ref[...] = hlast

    yg = (y_s[...] * (half_gate * (jnp.tanh(half_gate) + 1.0))).astype(BF16)
    out = x + jnp.dot(yg, w_out_ref[...], preferred_element_type=F32)
    o_ref[...] = jnp.swapaxes(out.reshape(tt, nb, d), 0, 1)


def _rnn_layer(x, conv0, h0, norm_g, w_in, conv_w, conv_b, wr, br, wi, bi, lam, w_out, *, reset, tt):
    nb, t, d = x.shape
    assert nb == SUBLANE and t % tt == 0 and tt % SUBLANE == 0
    d_rnn = lam.shape[-1]
    n_blk, blk, _ = wr.shape
    starts, gate_k = _gate_band_layout(d_rnn, blk)
    ncol = d_rnn // LANE
    rows = nb * tt
    halo = (CONV_W - 1) * nb

    dense_r = _block_diag_dense(wr)
    dense_i = _block_diag_dense(wi)
    wg = jnp.stack([
        jnp.concatenate([dense_r[ws:ws + gate_k, j * LANE:(j + 1) * LANE],
                         dense_i[ws:ws + gate_k, j * LANE:(j + 1) * LANE]], axis=1)
        for j, ws in enumerate(starts)])
    bg = jnp.concatenate([br.reshape(ncol, 1, LANE), bi.reshape(ncol, 1, LANE)], axis=2)
    wg = (0.5 * wg).astype(BF16)
    bg = 0.5 * bg
    gate_cols = jnp.arange(2 * d_rnn) >= d_rnn
    w_in = (w_in * jnp.where(gate_cols, 0.5, 1.0)).astype(BF16)

    full = lambda shape: pl.BlockSpec(shape, lambda i: (0,) * len(shape))
    kern = functools.partial(_rnn_layer_kernel, reset=reset, starts=starts, gate_k=gate_k)
    return pl.pallas_call(
        kern,
        grid=(t // tt,),
        in_specs=[
            pl.BlockSpec((nb, tt, d), lambda i: (0, i, 0)),
            full((halo, d_rnn)), full((nb, d_rnn)), full((1, d)),
            full((d, 2 * d_rnn)), full((CONV_W, d_rnn)), full((1, d_rnn)),
            full((ncol, gate_k, 2 * LANE)), full((ncol, 1, 2 * LANE)), full((1, d_rnn)),
            full((d_rnn, d)),
        ],
        out_specs=[
            pl.BlockSpec((nb, tt, d), lambda i: (0, i, 0)),
            full((halo, d_rnn)), full((nb, d_rnn)),
        ],
        out_shape=[
            jax.ShapeDtypeStruct((nb, t, d), F32),
            jax.ShapeDtypeStruct((halo, d_rnn), F32),
            jax.ShapeDtypeStruct((nb, d_rnn), F32),
        ],
        scratch_shapes=[
            pltpu.VMEM((halo + rows, d_rnn), F32),
            pltpu.VMEM((rows, d_rnn), F32),
            pltpu.VMEM((rows, d_rnn), F32),
            pltpu.VMEM((rows, d_rnn), F32),
            pltpu.VMEM((nb, d_rnn), F32),
        ],
        compiler_params=pltpu.CompilerParams(
            dimension_semantics=("arbitrary",), vmem_limit_bytes=VMEM_LIMIT_BYTES),
        name="rnn_layer_reset" if reset else "rnn_layer_state",
    )(x, conv0, h0, norm_g.reshape(1, d), w_in, conv_w, conv_b.reshape(1, d_rnn),
      wg, bg, lam.reshape(1, d_rnn), w_out.astype(BF16))


def _group_sum_matrices(width):
    groups = width // HEAD_DIM
    assert groups <= LANE
    gather = np.zeros((width, LANE), np.float32)
    gather[np.arange(width), np.arange(width) // HEAD_DIM] = 1.0
    expand = np.concatenate([gather.T, gather.T], axis=0)
    return jnp.asarray(gather, BF16), jnp.asarray(expand, BF16)


def _group_rms_scale(x, gather, expand):
    ssum = jnp.dot((x * x).astype(BF16), gather, preferred_element_type=F32)
    inv = lax.rsqrt(ssum * (1.0 / HEAD_DIM) + EPS)
    hi = inv.astype(BF16)
    lo = (inv - hi.astype(F32)).astype(BF16)
    return jnp.dot(jnp.concatenate([hi, lo], axis=1), expand, preferred_element_type=F32)


def _rope_slab(xh, cos_t, sin_t):
    lane = lax.broadcasted_iota(jnp.int32, xh.shape, 1)
    first_half = (lane % HEAD_DIM) < (HEAD_DIM // 2)
    partner = jnp.where(first_half, pltpu.roll(xh, LANE - HEAD_DIM // 2, 1),
                        pltpu.roll(xh, HEAD_DIM // 2, 1))
    return xh * cos_t + partner * sin_t


def _proj_kernel(x_ref, gkv_ref, gb_ref, kvw_ref, bw_ref, gk_ref, gq_ref, cos_ref, sin_ref,
                 gather_ref, expand_ref, *out_refs, prompt, kt):
    if prompt:
        k_ref, v_ref, khm_ref, vt_ref, qt_ref, sg_ref = out_refs
    else:
        k_ref, v_ref, q_ref, sg_ref = out_refs
    rows = x_ref.shape[1]
    w = gk_ref.shape[1]
    nh = w // V_DIM
    sub = kt if prompt else rows
    gather, expand = gather_ref[...], expand_ref[...]
    if prompt:
        pad_row = lax.broadcasted_iota(jnp.int32, (BF16_ROWS, kt), 0)
        ones_row = jnp.where(pad_row == 0, 1.0, 0.0).astype(BF16)
    for c in range(rows // sub):
        rs = slice(c * sub, (c + 1) * sub)
        x = x_ref[0, rs, :]
        ms = jnp.mean(x * x, axis=-1, keepdims=True)
        xh = x * lax.rsqrt(ms + EPS)
        kv = jnp.dot((xh * gkv_ref[...]).astype(BF16), kvw_ref[...], preferred_element_type=F32)
        qg = jnp.dot((xh * gb_ref[...]).astype(BF16), bw_ref[...], preferred_element_type=F32)
        k_raw, v = kv[:, :w], kv[:, w:]
        q_raw, half_gate = qg[:, :w], qg[:, w:]
        kn = k_raw * _group_rms_scale(k_raw, gather, expand) * gk_ref[...]
        qn = q_raw * _group_rms_scale(q_raw, gather, expand) * gq_ref[...]
        sg = half_gate * (jnp.tanh(half_gate) + 1.0)
        cos_t, sin_t = cos_ref[rs, :], sin_ref[rs, :]
        if not prompt:
            sg_ref[0] = sg
        for h in range(nh):
            cs = slice(h * V_DIM, (h + 1) * V_DIM)
            kh = _rope_slab(kn[:, cs], cos_t, sin_t)
            qh = _rope_slab(qn[:, cs], cos_t, sin_t)
            k_ref[0, pl.ds(c * sub * nh + h, sub, stride=nh), :] = kh
            v_ref[0, pl.ds(c * sub * nh + h, sub, stride=nh), :] = v[:, cs]
            if prompt:
                khm_ref[0, h, rs, :] = kh.astype(BF16)
                sg_ref[0, h, rs, :] = sg[:, cs].astype(BF16)
                vt_ref[0, h, c, 0:V_DIM, :] = v[:, cs].T.astype(BF16)
                vt_ref[0, h, c, V_DIM:V_DIM + BF16_ROWS, :] = ones_row
                qt_ref[0, h, c] = qh.T.astype(BF16)
            else:
                q_ref[0, :, cs] = qh


def _rope_tables(pos):
    half = HEAD_DIM // 2
    inv = ROPE_THETA ** (-jnp.arange(half, dtype=F32) / half)
    ang = pos.astype(F32)[:, None] * inv[None, :]
    cos, sin = jnp.cos(ang), jnp.sin(ang)
    reps = LANE // HEAD_DIM
    return (jnp.tile(jnp.concatenate([cos, cos], axis=1), (1, reps)),
            jnp.tile(jnp.concatenate([-sin, sin], axis=1), (1, reps)))


def _proj(x, pos, kv_norm, b_norm, kv_w, b_w_in, k_norm, q_norm, *, prompt, rows, kt):
    nb, t, d = x.shape
    w = kv_w.shape[1] // 2
    nh = w // V_DIM
    assert t % rows == 0 and (not prompt or rows % kt == 0)
    cos_t, sin_t = _rope_tables(pos)
    gather, expand = _group_sum_matrices(w)
    reps = w // HEAD_DIM
    gk = jnp.tile(k_norm.astype(F32), reps).reshape(1, w)
    gq = (jnp.tile(q_norm.astype(F32), reps) * (HEAD_DIM ** -0.5 * LOG2E)).reshape(1, w)
    b_w_half = (b_w_in * jnp.where(jnp.arange(2 * w) >= w, 0.5, 1.0)).astype(BF16)

    full = lambda shape: pl.BlockSpec(shape, lambda b, i: (0,) * len(shape))
    tok = pl.BlockSpec((1, rows, w), lambda b, i: (b, i, 0))
    tok_heads = pl.BlockSpec((1, rows * nh, V_DIM), lambda b, i: (b, i, 0))
    out_specs = [tok_heads, tok_heads]
    out_shape = [jax.ShapeDtypeStruct((nb, t * nh, V_DIM), F32)] * 2
    if prompt:
        nc = rows // kt
        hm = pl.BlockSpec((1, nh, rows, V_DIM), lambda b, i: (b, 0, i, 0))
        tr = lambda r: pl.BlockSpec((1, nh, nc, r, kt), lambda b, i: (b, 0, i, 0, 0))
        out_specs += [hm, tr(V_DIM + BF16_ROWS), tr(V_DIM), hm]
        out_shape += [jax.ShapeDtypeStruct((nb, nh, t, V_DIM), BF16),
                      jax.ShapeDtypeStruct((nb, nh, t // kt, V_DIM + BF16_ROWS, kt), BF16),
                      jax.ShapeDtypeStruct((nb, nh, t // kt, V_DIM, kt), BF16),
                      jax.ShapeDtypeStruct((nb, nh, t, V_DIM), BF16)]
    else:
        out_specs += [tok, tok]
        out_shape += [jax.ShapeDtypeStruct((nb, t, w), F32)] * 2
    return pl.pallas_call(
        functools.partial(_proj_kernel, prompt=prompt, kt=kt),
        grid=(nb, t // rows),
        in_specs=[
            pl.BlockSpec((1, rows, d), lambda b, i: (b, i, 0)),
            full((1, d)), full((1, d)), full((d, 2 * w)), full((d, 2 * w)),
            full((1, w)), full((1, w)),
            pl.BlockSpec((rows, LANE), lambda b, i: (i, 0)),
            pl.BlockSpec((rows, LANE), lambda b, i: (i, 0)),
            full((w, LANE)), full((2 * LANE, w)),
        ],
        out_specs=out_specs,
        out_shape=out_shape,
        compiler_params=pltpu.CompilerParams(
            dimension_semantics=("parallel", "parallel"), vmem_limit_bytes=VMEM_LIMIT_BYTES),
        name="proj_prompt" if prompt else "proj_sample",
    )(x, kv_norm.reshape(1, d), b_norm.reshape(1, d), kv_w.astype(BF16), b_w_half,
      gk, gq, cos_t, sin_t, gather, expand)


def _diff_lambda(lq1, lk1, lq2, lk2, lam_init):
    return (jnp.exp(jnp.sum(lq1 * lk1, axis=-1, keepdims=True))
            - jnp.exp(jnp.sum(lq2 * lk2, axis=-1, keepdims=True)) + lam_init)


def _prompt_attn_kernel(qt_ref, khm_ref, vt_ref, sg_ref, x_ref, w_out_ref, hg_ref,
                        lq1_ref, lk1_ref, lq2_ref, lk2_ref, gk_ref, o_ref,
                        qm_s, shift_s, acc_s, p_s, oh_s, *, lam_init):
    nh, vd, tq = qt_ref.shape[1], qt_ref.shape[3], qt_ref.shape[4]
    qi = pl.program_id(1)
    chains = [(h, c) for h in range(nh) for c in range(2)]

    key_norm_bound = (math.sqrt(HEAD_DIM) * KEY_NORM_SLACK
                      * jnp.max(jnp.abs(gk_ref[...]), axis=-1, keepdims=True))
    feat = lax.broadcasted_iota(jnp.int32, (vd, tq), 0)
    for h in range(nh):
        qt = qt_ref[0, h, 0]
        zero = jnp.zeros_like(qt)
        for c in range(2):
            qm = jnp.where((feat >= HEAD_DIM) if c else (feat < HEAD_DIM), qt, zero)
            qm_s[h, c] = qm
            qf = qm.astype(F32)
            shift_s[h, c] = jnp.sqrt(jnp.sum(qf * qf, axis=0, keepdims=True)) * key_norm_bound

    def visible_mask():
        krow = lax.broadcasted_iota(jnp.int32, (tq, tq), 0)
        qcol = lax.broadcasted_iota(jnp.int32, (tq, tq), 1)
        return (krow // CHUNK) <= (qcol // CHUNK)

    def key_tile(h, kt):
        return khm_ref[0, h, pl.ds(pl.multiple_of(kt * tq, tq), tq), :]

    def epilogue():
        lam = _diff_lambda(lq1_ref[...], lk1_ref[...], lq2_ref[...], lk2_ref[...], lam_init)
        for h in range(nh):
            w1 = 1.0 / acc_s[h, 0, vd:vd + 1, :]
            w2 = lam / acc_s[h, 1, vd:vd + 1, :]
            ot = acc_s[h, 0, 0:vd, :] * w1 - acc_s[h, 1, 0:vd, :] * w2
            ms = jnp.mean(ot * ot, axis=0, keepdims=True)
            on = (ot * lax.rsqrt(ms + EPS)) * hg_ref[...]
            oh_s[:, h * vd:(h + 1) * vd] = (on.T * sg_ref[0, h].astype(F32)).astype(BF16)
        o_ref[0] = x_ref[0] + jnp.dot(oh_s[...], w_out_ref[...], preferred_element_type=F32)

    def tile_update(kt, masked):
        if masked:
            visible = visible_mask()
        for h, c in chains:
            s = jnp.dot(key_tile(h, kt), qm_s[h, c], preferred_element_type=F32)
            p = jnp.exp2(s - shift_s[h, c])
            if masked:
                p = jnp.where(visible, p, 0.0)
            p_s[h, c] = p.astype(BF16)
        for h, c in chains:
            acc_s[h, c] += jnp.dot(vt_ref[0, h, kt], p_s[h, c], preferred_element_type=F32)

    def pair_body(i, carry):
        tile_update(2 * i, False)
        tile_update(2 * i + 1, False)
        return carry

    acc_s[...] = jnp.zeros(acc_s.shape, F32)
    lax.fori_loop(0, qi // 2, pair_body, 0)

    @pl.when(qi % 2 == 1)
    def _():
        tile_update(qi - 1, False)

    tile_update(qi, True)
    epilogue()

    denom_min = jnp.min(acc_s[:, :, vd:vd + 1, :])

    @pl.when(jnp.logical_not(denom_min >= DENOM_FLOOR))
    def _():
        visible = visible_mask()

        def head_body(h, carry):
            def tile_step(kt, state, masked):
                new = []
                for c, (m, acc) in enumerate(state):
                    s = jnp.dot(key_tile(h, kt), qm_s[h, c], preferred_element_type=F32)
                    if masked:
                        s = jnp.where(visible, s, NEG_BIG)
                    m_new = jnp.maximum(m, jnp.max(s, axis=0, keepdims=True))
                    p = jnp.exp2(s - m_new).astype(BF16)
                    acc_new = jnp.exp2(m - m_new) * acc + jnp.dot(
                        vt_ref[0, h, kt], p, preferred_element_type=F32)
                    new.append((m_new, acc_new))
                return tuple(new)

            init = tuple((jnp.full((1, tq), NEG_BIG, F32), jnp.zeros(acc_s.shape[2:], F32))
                         for _ in range(2))
            state = lax.fori_loop(0, qi, lambda kt, st: tile_step(kt, st, False), init)
            state = tile_step(qi, state, True)
            for c in range(2):
                acc_s[h, c] = state[c][1]
            return carry

        lax.fori_loop(0, nh, head_body, 0)
        epilogue()


def _prompt_attn(qt, khm, vt, sg, x, w_out, head_g, lq1, lk1, lq2, lk2, k_norm, *, lam_init):
    nb, nh, nq, vd, tq = qt.shape
    vde = vt.shape[3]
    t = khm.shape[2]
    d = x.shape[-1]
    w = nh * vd
    hg = jnp.broadcast_to((head_g.astype(F32) * (1.0 - lam_init))[:, None], (vd, tq))
    full = lambda shape: pl.BlockSpec(shape, lambda b, i: (0,) * len(shape))
    row = lambda a: a.reshape(1, -1).astype(F32)
    return pl.pallas_call(
        functools.partial(_prompt_attn_kernel, lam_init=lam_init),
        grid=(nb, nq),
        in_specs=[
            pl.BlockSpec((1, nh, 1, vd, tq), lambda b, i: (b, 0, i, 0, 0)),
            pl.BlockSpec((1, nh, t, vd), lambda b, i: (b, 0, 0, 0)),
            pl.BlockSpec((1, nh, nq, vde, tq), lambda b, i: (b, 0, 0, 0, 0)),
            pl.BlockSpec((1, nh, tq, vd), lambda b, i: (b, 0, i, 0)),
            pl.BlockSpec((1, tq, d), lambda b, i: (b, i, 0)),
            full((w, d)), full((vd, tq)),
            full((1, HEAD_DIM)), full((1, HEAD_DIM)), full((1, HEAD_DIM)), full((1, HEAD_DIM)),
            full((1, HEAD_DIM)),
        ],
        out_specs=pl.BlockSpec((1, tq, d), lambda b, i: (b, i, 0)),
        out_shape=jax.ShapeDtypeStruct((nb, t, d), F32),
        scratch_shapes=[
            pltpu.VMEM((nh, 2, vd, tq), BF16),
            pltpu.VMEM((nh, 2, 1, tq), F32),
            pltpu.VMEM((nh, 2, vde, tq), F32),
            pltpu.VMEM((nh, 2, tq, tq), BF16),
            pltpu.VMEM((tq, w), BF16),
        ],
        compiler_params=pltpu.CompilerParams(
            dimension_semantics=("parallel", "arbitrary"), vmem_limit_bytes=VMEM_LIMIT_BYTES),
        name="attn_prompt",
    )(qt, khm, vt, sg, x, w_out.astype(BF16), hg, row(lq1), row(lk1), row(lq2), row(lk2),
      row(k_norm))


def _sample_attn_kernel(q_ref, kn_ref, vn_ref, ck_ref, cv_ref, sg_ref, x_ref, w_out_ref, hg_ref,
                        lq1_ref, lk1_ref, lq2_ref, lk2_ref, o_ref, *, lam_init):
    q = q_ref[0]
    tq, w = q.shape
    nh = w // V_DIM
    past = ck_ref.shape[1] // nh
    lam = _diff_lambda(lq1_ref[...], lk1_ref[...], lq2_ref[...], lk2_ref[...], lam_init)
    lane = lax.broadcasted_iota(jnp.int32, (tq, V_DIM), 1)
    contract_last = (((1,), (1,)), ((), ()))
    outs = []
    for h in range(nh):
        cs = slice(h * V_DIM, (h + 1) * V_DIM)
        qh = q[:, cs]
        ck = ck_ref[0, pl.ds(h, past, stride=nh), :].astype(BF16)
        cv = cv_ref[0, pl.ds(h, past, stride=nh), :].astype(BF16)
        kn = kn_ref[0, pl.ds(h, tq, stride=nh), :].astype(BF16)
        vn = vn_ref[0, pl.ds(h, tq, stride=nh), :].astype(BF16)
        q2 = jnp.concatenate([jnp.where(lane < HEAD_DIM, qh, 0.0),
                              jnp.where(lane >= HEAD_DIM, qh, 0.0)], axis=0).astype(BF16)
        s_old = lax.dot_general(q2, ck, contract_last, preferred_element_type=F32)
        s_new = lax.dot_general(q2, kn, contract_last, preferred_element_type=F32)
        m = jnp.maximum(jnp.max(s_old, axis=-1, keepdims=True),
                        jnp.max(s_new, axis=-1, keepdims=True))
        p_old = jnp.exp2(s_old - m)
        p_new = jnp.exp2(s_new - m)
        l = jnp.sum(p_old, axis=-1, keepdims=True) + jnp.sum(p_new, axis=-1, keepdims=True)
        pv = (jnp.dot(p_old.astype(BF16), cv, preferred_element_type=F32)
              + jnp.dot(p_new.astype(BF16), vn, preferred_element_type=F32)) / l
        oh = pv[0:tq] - lam * pv[tq:2 * tq]
        ms = jnp.mean(oh * oh, axis=-1, keepdims=True)
        outs.append(oh * lax.rsqrt(ms + EPS) * hg_ref[...] * (1.0 - lam_init))
    o = (jnp.concatenate(outs, axis=1) * sg_ref[0]).astype(BF16)
    o_ref[0] = x_ref[0] + jnp.dot(o, w_out_ref[...], preferred_element_type=F32)


def _sample_attn(q, k_new, v_new, cache_k, cache_v, sg, x, w_out, head_g, lq1, lk1, lq2, lk2, *, lam_init):
    nb, tq, w = q.shape
    past = cache_k.shape[1]
    d = x.shape[-1]
    full = lambda shape: pl.BlockSpec(shape, lambda b: (0,) * len(shape))
    tok = lambda width: pl.BlockSpec((1, tq, width), lambda b: (b, 0, 0))
    nh = w // V_DIM
    old = pl.BlockSpec((1, past * nh, V_DIM), lambda b: (b, 0, 0))
    new = pl.BlockSpec((1, tq * nh, V_DIM), lambda b: (b, 0, 0))
    row = lambda a: a.reshape(1, -1).astype(F32)
    return pl.pallas_call(
        functools.partial(_sample_attn_kernel, lam_init=lam_init),
        grid=(nb,),
        in_specs=[tok(w), new, new, old, old, tok(w), tok(d),
                  full((w, d)), full((1, V_DIM)),
                  full((1, HEAD_DIM)), full((1, HEAD_DIM)), full((1, HEAD_DIM)), full((1, HEAD_DIM))],
        out_specs=tok(d),
        out_shape=jax.ShapeDtypeStruct((nb, tq, d), F32),
        compiler_params=pltpu.CompilerParams(
            dimension_semantics=("parallel",), vmem_limit_bytes=VMEM_LIMIT_BYTES),
        name="attn_sample",
    )(q, k_new, v_new, cache_k.reshape(nb, past * nh, V_DIM), cache_v.reshape(nb, past * nh, V_DIM), sg, x,
      w_out.astype(BF16), row(head_g), row(lq1), row(lk1), row(lq2), row(lk2))


def _pick_tile(n, target):
    tile = min(n, target)
    while n % tile:
        tile //= 2
    return tile


def kernel(x_prompt, x_sample, state_conv, state_h, cache_k, cache_v, a_norm, a_w_in, a_conv_w, a_conv_b, a_gate_r_w, a_gate_r_b, a_gate_i_w, a_gate_i_b, a_lambda, a_w_out, kv_norm, kv_w, k_norm, b_norm, b_w_in, b_q_norm, b_lambda_q1, b_lambda_k1, b_lambda_q2, b_lambda_k2, b_head_norm, b_w_out):
    n_a, n_b = a_norm.shape[0], b_norm.shape[0]
    assert n_a == 1 and n_b == 1, "one recurrent layer followed by one attention layer"
    nb, t, d = x_prompt.shape
    nbs, ts, _ = x_sample.shape
    d_rnn = a_lambda.shape[-1]
    w = kv_w.shape[1] // 2
    nh = w // V_DIM
    past = cache_k.shape[1]
    halo = CONV_W - 1
    lam_init = 0.8 - 0.6 * math.exp(-0.3 * n_a)
    a_args = (a_norm[0], a_w_in[0], a_conv_w[0], a_conv_b[0], a_gate_r_w[0], a_gate_r_b[0],
              a_gate_i_w[0], a_gate_i_b[0], a_lambda[0], a_w_out[0])
    lam_args = (b_lambda_q1[0], b_lambda_k1[0], b_lambda_q2[0], b_lambda_k2[0])

    def conv_state_out(c, n):
        return jnp.swapaxes(c.reshape(halo, n, d_rnn), 0, 1)[None]

    x1, conv_p, h_p = _rnn_layer(
        x_prompt, jnp.zeros((halo * nb, d_rnn), F32), jnp.zeros((nb, d_rnn), F32), *a_args,
        reset=True, tt=_pick_tile(t, 64))
    tq = _pick_tile(t, 256)
    k_p, v_p, khm, vt, qt, sg = _proj(
        x1, jnp.arange(t, dtype=jnp.int32), kv_norm, b_norm[0], kv_w, b_w_in[0], k_norm, b_q_norm[0],
        prompt=True, rows=_pick_tile(t, 512), kt=tq)
    y_p = _prompt_attn(qt, khm, vt, sg, x1, b_w_out[0], b_head_norm[0], *lam_args, k_norm,
                       lam_init=lam_init)

    conv0 = jnp.swapaxes(state_conv[0], 0, 1).reshape(halo * nbs, d_rnn)
    x1s, conv_s, h_s = _rnn_layer(x_sample, conv0, state_h[0], *a_args, reset=False, tt=ts)
    k_s, v_s, q_s, sg_s = _proj(
        x1s, past + jnp.arange(ts, dtype=jnp.int32), kv_norm, b_norm[0], kv_w, b_w_in[0], k_norm,
        b_q_norm[0], prompt=False, rows=ts, kt=ts)
    y_s = _sample_attn(q_s, k_s, v_s, cache_k, cache_v, sg_s, x1s, b_w_out[0], b_head_norm[0],
                       *lam_args, lam_init=lam_init)

    return (y_p, y_s,
            conv_state_out(conv_p, nb), h_p[None],
            k_p.reshape(nb, t, nh, V_DIM), v_p.reshape(nb, t, nh, V_DIM),
            conv_state_out(conv_s, nbs), h_s[None],
            k_s.reshape(nbs, ts, nh, V_DIM), v_s.reshape(nbs, ts, nh, V_DIM))
```

```python
import functools
import math

import numpy as np
import jax
import jax.numpy as jnp
from jax import lax
from jax.experimental import pallas as pl
from jax.experimental.pallas import tpu as pltpu

LANE = 128
SUBLANE = 8
VMEM_LIMIT_BYTES = 56 * 1024 * 1024
VMEM_LIMIT_PROJ_BYTES = 58 * 1024 * 1024

CHUNK = 64
CONV_W = 4
LRU_C = 8.0
HEAD_DIM = 64
V_DIM = 2 * HEAD_DIM
ROPE_THETA = 10000.0
EPS = 1e-6
LOG2E = 1.4426950408889634
NEG_BIG = -1e30
BF16_ROWS = 16
KEY_NORM_SLACK = 1.0 + 2.0 ** -6
DENOM_FLOOR = 2.0 ** -100

F32 = jnp.float32
BF16 = jnp.bfloat16


def _gate_band_layout(d_rnn, blk):
    ncol = d_rnn // LANE
    spans = []
    for j in range(ncol):
        lo = ((LANE * j) // blk) * blk
        hi = ((LANE * j + LANE - 1) // blk + 1) * blk
        spans.append((lo, hi))
    width = max(-(-(hi - (lo // LANE) * LANE) // LANE) * LANE for lo, hi in spans)
    starts = []
    for lo, hi in spans:
        ws = min((lo // LANE) * LANE, d_rnn - width)
        assert ws <= lo and hi <= ws + width
        starts.append(ws)
    return tuple(starts), width


def _block_diag_dense(w):
    n, blk, _ = w.shape
    return jnp.concatenate(
        [jnp.pad(w[i], ((0, 0), (blk * i, blk * (n - 1 - i)))) for i in range(n)], axis=0)


def _rnn_layer_kernel(x_ref, conv0_ref, h0_ref, g_ref, w_in_ref, cw_ref, cb_ref, wg_ref, bg_ref,
                      lam_ref, w_out_ref, o_ref, conv_o_ref, h_o_ref,
                      ubuf, a_s, b_s, y_s, hc, *, reset, starts, gate_k):
    nb, tt, d = x_ref.shape
    rows = nb * tt
    d_rnn = lam_ref.shape[1]
    halo = (CONV_W - 1) * nb
    step = pl.program_id(0)

    @pl.when(step == 0)
    def _():
        ubuf[0:halo, :] = conv0_ref[...]
        hc[...] = h0_ref[...]

    x = jnp.swapaxes(x_ref[...], 0, 1).reshape(rows, d)
    ms = jnp.mean(x * x, axis=-1, keepdims=True)
    xn = (x * lax.rsqrt(ms + EPS) * g_ref[...]).astype(BF16)
    h = jnp.dot(xn, w_in_ref[...], preferred_element_type=F32)
    ubuf[halo:halo + rows, :] = h[:, :d_rnn]
    half_gate = h[:, d_rnn:]

    u = cb_ref[...] + cw_ref[CONV_W - 1:CONV_W, :] * ubuf[halo:halo + rows, :]
    for j in range(CONV_W - 1):
        u = u + cw_ref[j:j + 1, :] * ubuf[j * nb:j * nb + rows, :]
    ubuf[0:halo, :] = ubuf[rows:rows + halo, :]
    conv_o_ref[...] = ubuf[0:halo, :]

    lam = lam_ref[...]
    softplus_neg = jnp.maximum(-lam, 0.0) + jnp.log1p(jnp.exp(-jnp.abs(lam)))
    c2 = (-0.5 * LOG2E * LRU_C) * softplus_neg

    u_bf = u.astype(BF16)
    for j, ws in enumerate(starts):
        cs = slice(j * LANE, (j + 1) * LANE)
        ri = jnp.dot(u_bf[:, ws:ws + gate_k], wg_ref[j], preferred_element_type=F32) + bg_ref[j]
        t_r = jnp.tanh(ri[:, :LANE])
        t_i = jnp.tanh(ri[:, LANE:])
        a = jnp.exp2(c2[:, cs] * t_r + c2[:, cs])
        y = 1.0 - a * a
        mult = jnp.where(y > 0.0, y * lax.rsqrt(y), 0.0)
        if reset:
            head = jnp.where(step == 0, 1.0, mult[0:nb])
            mult = jnp.concatenate([head, mult[nb:]], axis=0)
        a_s[:, cs] = a
        b_s[:, cs] = (mult * u[:, cs]) * (0.5 * t_i + 0.5)

    def scan_step(t, hcur):
        r0 = pl.multiple_of(t * nb, SUBLANE)
        hn = a_s[pl.ds(r0, nb), :] * hcur + b_s[pl.ds(r0, nb), :]
        y_s[pl.ds(r0, nb), :] = hn
        return hn

    hlast = lax.fori_loop(0, tt, scan_step, hc[...], unroll=min(tt, 8))
    hc[...] = hlast
    h_o_ref[...] = hlast

    yg = (y_s[...] * (half_gate * (jnp.tanh(half_gate) + 1.0))).astype(BF16)
    out = x + jnp.dot(yg, w_out_ref[...], preferred_element_type=F32)
    o_ref[...] = jnp.swapaxes(out.reshape(tt, nb, d), 0, 1)


def _rnn_layer(x, conv0, h0, norm_g, w_in, conv_w, conv_b, wr, br, wi, bi, lam, w_out, *, reset, tt):
    nb, t, d = x.shape
    assert nb == SUBLANE and t % tt == 0 and tt % SUBLANE == 0
    d_rnn = lam.shape[-1]
    n_blk, blk, _ = wr.shape
    starts, gate_k = _gate_band_layout(d_rnn, blk)
    ncol = d_rnn // LANE
    rows = nb * tt
    halo = (CONV_W - 1) * nb

    dense_r = _block_diag_dense(wr)
    dense_i = _block_diag_dense(wi)
    wg = jnp.stack([
        jnp.concatenate([dense_r[ws:ws + gate_k, j * LANE:(j + 1) * LANE],
                         dense_i[ws:ws + gate_k, j * LANE:(j + 1) * LANE]], axis=1)
        for j, ws in enumerate(starts)])
    bg = jnp.concatenate([br.reshape(ncol, 1, LANE), bi.reshape(ncol, 1, LANE)], axis=2)
    wg = (0.5 * wg).astype(BF16)
    bg = 0.5 * bg
    gate_cols = jnp.arange(2 * d_rnn) >= d_rnn
    w_in = (w_in * jnp.where(gate_cols, 0.5, 1.0)).astype(BF16)

    full = lambda shape: pl.BlockSpec(shape, lambda i: (0,) * len(shape))
    kern = functools.partial(_rnn_layer_kernel, reset=reset, starts=starts, gate_k=gate_k)
    return pl.pallas_call(
        kern,
        grid=(t // tt,),
        in_specs=[
            pl.BlockSpec((nb, tt, d), lambda i: (0, i, 0)),
            full((halo, d_rnn)), full((nb, d_rnn)), full((1, d)),
            full((d, 2 * d_rnn)), full((CONV_W, d_rnn)), full((1, d_rnn)),
            full((ncol, gate_k, 2 * LANE)), full((ncol, 1, 2 * LANE)), full((1, d_rnn)),
            full((d_rnn, d)),
        ],
        out_specs=[
            pl.BlockSpec((nb, tt, d), lambda i: (0, i, 0)),
            full((halo, d_rnn)), full((nb, d_rnn)),
        ],
        out_shape=[
            jax.ShapeDtypeStruct((nb, t, d), F32),
            jax.ShapeDtypeStruct((halo, d_rnn), F32),
            jax.ShapeDtypeStruct((nb, d_rnn), F32),
        ],
        scratch_shapes=[
            pltpu.VMEM((halo + rows, d_rnn), F32),
            pltpu.VMEM((rows, d_rnn), F32),
            pltpu.VMEM((rows, d_rnn), F32),
            pltpu.VMEM((rows, d_rnn), F32),
            pltpu.VMEM((nb, d_rnn), F32),
        ],
        compiler_params=pltpu.CompilerParams(
            dimension_semantics=("arbitrary",), vmem_limit_bytes=VMEM_LIMIT_BYTES),
        name="rnn_layer_reset" if reset else "rnn_layer_state",
    )(x, conv0, h0, norm_g.reshape(1, d), w_in, conv_w, conv_b.reshape(1, d_rnn),
      wg, bg, lam.reshape(1, d_rnn), w_out.astype(BF16))


def _group_sum_matrices(width):
    groups = width // HEAD_DIM
    assert groups <= LANE
    gather = np.zeros((width, LANE), np.float32)
    gather[np.arange(width), np.arange(width) // HEAD_DIM] = 1.0
    expand = np.concatenate([gather.T, gather.T], axis=0)
    return jnp.asarray(gather, BF16), jnp.asarray(expand, BF16)


def _group_rms_scale(x, gather, expand):
    ssum = jnp.dot((x * x).astype(BF16), gather, preferred_element_type=F32)
    inv = lax.rsqrt(ssum * (1.0 / HEAD_DIM) + EPS)
    hi = inv.astype(BF16)
    lo = (inv - hi.astype(F32)).astype(BF16)
    return jnp.dot(jnp.concatenate([hi, lo], axis=1), expand, preferred_element_type=F32)


def _rope_slab(xh, cos_t, sin_t):
    lane = lax.broadcasted_iota(jnp.int32, xh.shape, 1)
    first_half = (lane % HEAD_DIM) < (HEAD_DIM // 2)
    partner = jnp.where(first_half, pltpu.roll(xh, LANE - HEAD_DIM // 2, 1),
                        pltpu.roll(xh, HEAD_DIM // 2, 1))
    return xh * cos_t + partner * sin_t


def _proj_kernel(x_ref, gkv_ref, gb_ref, kvw_ref, bw_ref, gk_ref, gq_ref, cos_ref, sin_ref,
                 gather_ref, expand_ref, *out_refs, prompt, kt):
    if prompt:
        k_ref, v_ref, khm_ref, vt_ref, qt_ref, sg_ref = out_refs
    else:
        k_ref, v_ref, q_ref, sg_ref = out_refs
    rows = x_ref.shape[1]
    w = gk_ref.shape[1]
    nh = w // V_DIM
    sub = kt if prompt else rows
    gather, expand = gather_ref[...], expand_ref[...]
    if prompt:
        pad_row = lax.broadcasted_iota(jnp.int32, (BF16_ROWS, kt), 0)
        ones_row = jnp.where(pad_row == 0, 1.0, 0.0).astype(BF16)
    for c in range(rows // sub):
        rs = slice(c * sub, (c + 1) * sub)
        x = x_ref[0, rs, :]
        ms = jnp.mean(x * x, axis=-1, keepdims=True)
        xh = x * lax.rsqrt(ms + EPS)
        kv = jnp.dot((xh * gkv_ref[...]).astype(BF16), kvw_ref[...], preferred_element_type=F32)
        qg = jnp.dot((xh * gb_ref[...]).astype(BF16), bw_ref[...], preferred_element_type=F32)
        k_raw, v = kv[:, :w], kv[:, w:]
        q_raw, half_gate = qg[:, :w], qg[:, w:]
        kn = k_raw * _group_rms_scale(k_raw, gather, expand) * gk_ref[...]
        qn = q_raw * _group_rms_scale(q_raw, gather, expand) * gq_ref[...]
        sg = half_gate * (jnp.tanh(half_gate) + 1.0)
        cos_t, sin_t = cos_ref[rs, :], sin_ref[rs, :]
        if not prompt:
            sg_ref[0] = sg
        for h in range(nh):
            cs = slice(h * V_DIM, (h + 1) * V_DIM)
            kh = _rope_slab(kn[:, cs], cos_t, sin_t)
            qh = _rope_slab(qn[:, cs], cos_t, sin_t)
            k_ref[0, pl.ds(c * sub * nh + h, sub, stride=nh), :] = kh
            v_ref[0, pl.ds(c * sub * nh + h, sub, stride=nh), :] = v[:, cs]
            if prompt:
                khm_ref[0, h, rs, :] = kh.astype(BF16)
                sg_ref[0, h, rs, :] = sg[:, cs].astype(BF16)
                vt_ref[0, h, c, 0:V_DIM, :] = v[:, cs].T.astype(BF16)
                vt_ref[0, h, c, V_DIM:V_DIM + BF16_ROWS, :] = ones_row
                qt_ref[0, h, c] = qh.T.astype(BF16)
            else:
                q_ref[0, :, cs] = qh


def _rope_tables(pos):
    half = HEAD_DIM // 2
    inv = ROPE_THETA ** (-jnp.arange(half, dtype=F32) / half)
    ang = pos.astype(F32)[:, None] * inv[None, :]
    cos, sin = jnp.cos(ang), jnp.sin(ang)
    reps = LANE // HEAD_DIM
    return (jnp.tile(jnp.concatenate([cos, cos], axis=1), (1, reps)),
            jnp.tile(jnp.concatenate([-sin, sin], axis=1), (1, reps)))


def _proj(x, pos, kv_norm, b_norm, kv_w, b_w_in, k_norm, q_norm, *, prompt, rows, kt):
    nb, t, d = x.shape
    w = kv_w.shape[1] // 2
    nh = w // V_DIM
    assert t % rows == 0 and (not prompt or rows % kt == 0)
    cos_t, sin_t = _rope_tables(pos)
    gather, expand = _group_sum_matrices(w)
    reps = w // HEAD_DIM
    gk = jnp.tile(k_norm.astype(F32), reps).reshape(1, w)
    gq = (jnp.tile(q_norm.astype(F32), reps) * (HEAD_DIM ** -0.5 * LOG2E)).reshape(1, w)
    b_w_half = (b_w_in * jnp.where(jnp.arange(2 * w) >= w, 0.5, 1.0)).astype(BF16)

    full = lambda shape: pl.BlockSpec(shape, lambda b, i: (0,) * len(shape),
                                      pipeline_mode=pl.Buffered(1))
    tok = pl.BlockSpec((1, rows, w), lambda b, i: (b, i, 0))
    tok_heads = pl.BlockSpec((1, rows * nh, V_DIM), lambda b, i: (b, i, 0))
    out_specs = [tok_heads, tok_heads]
    out_shape = [jax.ShapeDtypeStruct((nb, t * nh, V_DIM), F32)] * 2
    if prompt:
        nc = rows // kt
        hm = pl.BlockSpec((1, nh, rows, V_DIM), lambda b, i: (b, 0, i, 0))
        tr = lambda r: pl.BlockSpec((1, nh, nc, r, kt), lambda b, i: (b, 0, i, 0, 0))
        out_specs += [hm, tr(V_DIM + BF16_ROWS), tr(V_DIM), hm]
        out_shape += [jax.ShapeDtypeStruct((nb, nh, t, V_DIM), BF16),
                      jax.ShapeDtypeStruct((nb, nh, t // kt, V_DIM + BF16_ROWS, kt), BF16),
                      jax.ShapeDtypeStruct((nb, nh, t // kt, V_DIM, kt), BF16),
                      jax.ShapeDtypeStruct((nb, nh, t, V_DIM), BF16)]
    else:
        out_specs += [tok, tok]
        out_shape += [jax.ShapeDtypeStruct((nb, t, w), F32)] * 2
    return pl.pallas_call(
        functools.partial(_proj_kernel, prompt=prompt, kt=kt),
        grid=(nb, t // rows),
        in_specs=[
            pl.BlockSpec((1, rows, d), lambda b, i: (b, i, 0)),
            full((1, d)), full((1, d)), full((d, 2 * w)), full((d, 2 * w)),
            full((1, w)), full((1, w)),
            pl.BlockSpec((rows, LANE), lambda b, i: (i, 0)),
            pl.BlockSpec((rows, LANE), lambda b, i: (i, 0)),
            full((w, LANE)), full((2 * LANE, w)),
        ],
        out_specs=out_specs,
        out_shape=out_shape,
        compiler_params=pltpu.CompilerParams(
            dimension_semantics=("parallel", "parallel"), vmem_limit_bytes=VMEM_LIMIT_PROJ_BYTES),
        name="proj_prompt" if prompt else "proj_sample",
    )(x, kv_norm.reshape(1, d), b_norm.reshape(1, d), kv_w.astype(BF16), b_w_half,
      gk, gq, cos_t, sin_t, gather, expand)


def _diff_lambda(lq1, lk1, lq2, lk2, lam_init):
    return (jnp.exp(jnp.sum(lq1 * lk1, axis=-1, keepdims=True))
            - jnp.exp(jnp.sum(lq2 * lk2, axis=-1, keepdims=True)) + lam_init)


def _prompt_attn_kernel(qt_ref, khm_ref, vt_ref, sg_ref, x_ref, w_out_ref, hg_ref,
                        lq1_ref, lk1_ref, lq2_ref, lk2_ref, gk_ref, o_ref,
                        qm_s, shift_s, acc_s, p_s, oh_s, *, lam_init):
    nh, vd, tq = qt_ref.shape[1], qt_ref.shape[3], qt_ref.shape[4]
    qi = pl.program_id(1)
    chains = [(h, c) for h in range(nh) for c in range(2)]

    key_norm_bound = (math.sqrt(HEAD_DIM) * KEY_NORM_SLACK
                      * jnp.max(jnp.abs(gk_ref[...]), axis=-1, keepdims=True))
    feat = lax.broadcasted_iota(jnp.int32, (vd, tq), 0)
    for h in range(nh):
        qt = qt_ref[0, h, 0]
        zero = jnp.zeros_like(qt)
        for c in range(2):
            qm = jnp.where((feat >= HEAD_DIM) if c else (feat < HEAD_DIM), qt, zero)
            qm_s[h, c] = qm
            qf = qm.astype(F32)
            shift_s[h, c] = jnp.sqrt(jnp.sum(qf * qf, axis=0, keepdims=True)) * key_norm_bound

    def visible_mask():
        krow = lax.broadcasted_iota(jnp.int32, (tq, tq), 0)
        qcol = lax.broadcasted_iota(jnp.int32, (tq, tq), 1)
        return (krow // CHUNK) <= (qcol // CHUNK)

    def key_tile(h, kt):
        return khm_ref[0, h, pl.ds(pl.multiple_of(kt * tq, tq), tq), :]

    def epilogue():
        lam = _diff_lambda(lq1_ref[...], lk1_ref[...], lq2_ref[...], lk2_ref[...], lam_init)
        for h in range(nh):
            w1 = 1.0 / acc_s[h, 0, vd:vd + 1, :]
            w2 = lam / acc_s[h, 1, vd:vd + 1, :]
            ot = acc_s[h, 0, 0:vd, :] * w1 - acc_s[h, 1, 0:vd, :] * w2
            ms = jnp.mean(ot * ot, axis=0, keepdims=True)
            on = (ot * lax.rsqrt(ms + EPS)) * hg_ref[...]
            oh_s[:, h * vd:(h + 1) * vd] = (on.T * sg_ref[0, h].astype(F32)).astype(BF16)
        o_ref[0] = x_ref[0] + jnp.dot(oh_s[...], w_out_ref[...], preferred_element_type=F32)

    def tile_update(kt, masked):
        if masked:
            visible = visible_mask()
        for h, c in chains:
            s = jnp.dot(key_tile(h, kt), qm_s[h, c], preferred_element_type=F32)
            p = jnp.exp2(s - shift_s[h, c])
            if masked:
                p = jnp.where(visible, p, 0.0)
            p_s[h, c] = p.astype(BF16)
        for h, c in chains:
            acc_s[h, c] += jnp.dot(vt_ref[0, h, kt], p_s[h, c], preferred_element_type=F32)

    def pair_body(i, carry):
        tile_update(2 * i, False)
        tile_update(2 * i + 1, False)
        return carry

    acc_s[...] = jnp.zeros(acc_s.shape, F32)
    lax.fori_loop(0, qi // 2, pair_body, 0)

    @pl.when(qi % 2 == 1)
    def _():
        tile_update(qi - 1, False)

    tile_update(qi, True)
    epilogue()

    denom_min = jnp.min(acc_s[:, :, vd:vd + 1, :])

    @pl.when(jnp.logical_not(denom_min >= DENOM_FLOOR))
    def _():
        visible = visible_mask()

        def head_body(h, carry):
            def tile_step(kt, state, masked):
                new = []
                for c, (m, acc) in enumerate(state):
                    s = jnp.dot(key_tile(h, kt), qm_s[h, c], preferred_element_type=F32)
                    if masked:
                        s = jnp.where(visible, s, NEG_BIG)
                    m_new = jnp.maximum(m, jnp.max(s, axis=0, keepdims=True))
                    p = jnp.exp2(s - m_new).astype(BF16)
                    acc_new = jnp.exp2(m - m_new) * acc + jnp.dot(
                        vt_ref[0, h, kt], p, preferred_element_type=F32)
                    new.append((m_new, acc_new))
                return tuple(new)

            init = tuple((jnp.full((1, tq), NEG_BIG, F32), jnp.zeros(acc_s.shape[2:], F32))
                         for _ in range(2))
            state = lax.fori_loop(0, qi, lambda kt, st: tile_step(kt, st, False), init)
            state = tile_step(qi, state, True)
            for c in range(2):
                acc_s[h, c] = state[c][1]
            return carry

        lax.fori_loop(0, nh, head_body, 0)
        epilogue()


def _prompt_attn(qt, khm, vt, sg, x, w_out, head_g, lq1, lk1, lq2, lk2, k_norm, *, lam_init):
    nb, nh, nq, vd, tq = qt.shape
    vde = vt.shape[3]
    t = khm.shape[2]
    d = x.shape[-1]
    w = nh * vd
    hg = jnp.broadcast_to((head_g.astype(F32) * (1.0 - lam_init))[:, None], (vd, tq))
    full = lambda shape: pl.BlockSpec(shape, lambda b, i: (0,) * len(shape))
    row = lambda a: a.reshape(1, -1).astype(F32)
    return pl.pallas_call(
        functools.partial(_prompt_attn_kernel, lam_init=lam_init),
        grid=(nb, nq),
        in_specs=[
            pl.BlockSpec((1, nh, 1, vd, tq), lambda b, i: (b, 0, i, 0, 0)),
            pl.BlockSpec((1, nh, t, vd), lambda b, i: (b, 0, 0, 0)),
            pl.BlockSpec((1, nh, nq, vde, tq), lambda b, i: (b, 0, 0, 0, 0)),
            pl.BlockSpec((1, nh, tq, vd), lambda b, i: (b, 0, i, 0)),
            pl.BlockSpec((1, tq, d), lambda b, i: (b, i, 0)),
            full((w, d)), full((vd, tq)),
            full((1, HEAD_DIM)), full((1, HEAD_DIM)), full((1, HEAD_DIM)), full((1, HEAD_DIM)),
            full((1, HEAD_DIM)),
        ],
        out_specs=pl.BlockSpec((1, tq, d), lambda b, i: (b, i, 0)),
        out_shape=jax.ShapeDtypeStruct((nb, t, d), F32),
        scratch_shapes=[
            pltpu.VMEM((nh, 2, vd, tq), BF16),
            pltpu.VMEM((nh, 2, 1, tq), F32),
            pltpu.VMEM((nh, 2, vde, tq), F32),
            pltpu.VMEM((nh, 2, tq, tq), BF16),
            pltpu.VMEM((tq, w), BF16),
        ],
        compiler_params=pltpu.CompilerParams(
            dimension_semantics=("parallel", "arbitrary"), vmem_limit_bytes=VMEM_LIMIT_BYTES),
        name="attn_prompt",
    )(qt, khm, vt, sg, x, w_out.astype(BF16), hg, row(lq1), row(lk1), row(lq2), row(lk2),
      row(k_norm))


def _sample_attn_kernel(q_ref, kn_ref, vn_ref, ck_ref, cv_ref, sg_ref, x_ref, w_out_ref, hg_ref,
                        lq1_ref, lk1_ref, lq2_ref, lk2_ref, o_ref, *, lam_init):
    q = q_ref[0]
    tq, w = q.shape
    nh = w // V_DIM
    past = ck_ref.shape[1] // nh
    lam = _diff_lambda(lq1_ref[...], lk1_ref[...], lq2_ref[...], lk2_ref[...], lam_init)
    lane = lax.broadcasted_iota(jnp.int32, (tq, V_DIM), 1)
    contract_last = (((1,), (1,)), ((), ()))
    outs = []
    for h in range(nh):
        cs = slice(h * V_DIM, (h + 1) * V_DIM)
        qh = q[:, cs]
        ck = ck_ref[0, pl.ds(h, past, stride=nh), :].astype(BF16)
        cv = cv_ref[0, pl.ds(h, past, stride=nh), :].astype(BF16)
        kn = kn_ref[0, pl.ds(h, tq, stride=nh), :].astype(BF16)
        vn = vn_ref[0, pl.ds(h, tq, stride=nh), :].astype(BF16)
        q2 = jnp.concatenate([jnp.where(lane < HEAD_DIM, qh, 0.0),
                              jnp.where(lane >= HEAD_DIM, qh, 0.0)], axis=0).astype(BF16)
        s_old = lax.dot_general(q2, ck, contract_last, preferred_element_type=F32)
        s_new = lax.dot_general(q2, kn, contract_last, preferred_element_type=F32)
        m = jnp.maximum(jnp.max(s_old, axis=-1, keepdims=True),
                        jnp.max(s_new, axis=-1, keepdims=True))
        p_old = jnp.exp2(s_old - m)
        p_new = jnp.exp2(s_new - m)
        l = jnp.sum(p_old, axis=-1, keepdims=True) + jnp.sum(p_new, axis=-1, keepdims=True)
        pv = (jnp.dot(p_old.astype(BF16), cv, preferred_element_type=F32)
              + jnp.dot(p_new.astype(BF16), vn, preferred_element_type=F32)) / l
        oh = pv[0:tq] - lam * pv[tq:2 * tq]
        ms = jnp.mean(oh * oh, axis=-1, keepdims=True)
        outs.append(oh * lax.rsqrt(ms + EPS) * hg_ref[...] * (1.0 - lam_init))
    o = (jnp.concatenate(outs, axis=1) * sg_ref[0]).astype(BF16)
    o_ref[0] = x_ref[0] + jnp.dot(o, w_out_ref[...], preferred_element_type=F32)


def _sample_attn(q, k_new, v_new, cache_k, cache_v, sg, x, w_out, head_g, lq1, lk1, lq2, lk2, *, lam_init):
    nb, tq, w = q.shape
    past = cache_k.shape[1]
    d = x.shape[-1]
    full = lambda shape: pl.BlockSpec(shape, lambda b: (0,) * len(shape))
    tok = lambda width: pl.BlockSpec((1, tq, width), lambda b: (b, 0, 0))
    nh = w // V_DIM
    old = pl.BlockSpec((1, past * nh, V_DIM), lambda b: (b, 0, 0))
    new = pl.BlockSpec((1, tq * nh, V_DIM), lambda b: (b, 0, 0))
    row = lambda a: a.reshape(1, -1).astype(F32)
    return pl.pallas_call(
        functools.partial(_sample_attn_kernel, lam_init=lam_init),
        grid=(nb,),
        in_specs=[tok(w), new, new, old, old, tok(w), tok(d),
                  full((w, d)), full((1, V_DIM)),
                  full((1, HEAD_DIM)), full((1, HEAD_DIM)), full((1, HEAD_DIM)), full((1, HEAD_DIM))],
        out_specs=tok(d),
        out_shape=jax.ShapeDtypeStruct((nb, tq, d), F32),
        compiler_params=pltpu.CompilerParams(
            dimension_semantics=("parallel",), vmem_limit_bytes=VMEM_LIMIT_BYTES),
        name="attn_sample",
    )(q, k_new, v_new, cache_k.reshape(nb, past * nh, V_DIM), cache_v.reshape(nb, past * nh, V_DIM), sg, x,
      w_out.astype(BF16), row(head_g), row(lq1), row(lk1), row(lq2), row(lk2))


def _pick_tile(n, target):
    tile = min(n, target)
    while n % tile:
        tile //= 2
    return tile


def kernel(x_prompt, x_sample, state_conv, state_h, cache_k, cache_v, a_norm, a_w_in, a_conv_w, a_conv_b, a_gate_r_w, a_gate_r_b, a_gate_i_w, a_gate_i_b, a_lambda, a_w_out, kv_norm, kv_w, k_norm, b_norm, b_w_in, b_q_norm, b_lambda_q1, b_lambda_k1, b_lambda_q2, b_lambda_k2, b_head_norm, b_w_out):
    n_a, n_b = a_norm.shape[0], b_norm.shape[0]
    assert n_a == 1 and n_b == 1, "one recurrent layer followed by one attention layer"
    nb, t, d = x_prompt.shape
    nbs, ts, _ = x_sample.shape
    d_rnn = a_lambda.shape[-1]
    w = kv_w.shape[1] // 2
    nh = w // V_DIM
    past = cache_k.shape[1]
    halo = CONV_W - 1
    lam_init = 0.8 - 0.6 * math.exp(-0.3 * n_a)
    a_args = (a_norm[0], a_w_in[0], a_conv_w[0], a_conv_b[0], a_gate_r_w[0], a_gate_r_b[0],
              a_gate_i_w[0], a_gate_i_b[0], a_lambda[0], a_w_out[0])
    lam_args = (b_lambda_q1[0], b_lambda_k1[0], b_lambda_q2[0], b_lambda_k2[0])

    def conv_state_out(c, n):
        return jnp.swapaxes(c.reshape(halo, n, d_rnn), 0, 1)[None]

    x1, conv_p, h_p = _rnn_layer(
        x_prompt, jnp.zeros((halo * nb, d_rnn), F32), jnp.zeros((nb, d_rnn), F32), *a_args,
        reset=True, tt=_pick_tile(t, 64))
    tq = _pick_tile(t, 256)
    k_p, v_p, khm, vt, qt, sg = _proj(
        x1, jnp.arange(t, dtype=jnp.int32), kv_norm, b_norm[0], kv_w, b_w_in[0], k_norm, b_q_norm[0],
        prompt=True, rows=_pick_tile(t, 1024), kt=tq)
    y_p = _prompt_attn(qt, khm, vt, sg, x1, b_w_out[0], b_head_norm[0], *lam_args, k_norm,
                       lam_init=lam_init)

    conv0 = jnp.swapaxes(state_conv[0], 0, 1).reshape(halo * nbs, d_rnn)
    x1s, conv_s, h_s = _rnn_layer(x_sample, conv0, state_h[0], *a_args, reset=False, tt=ts)
    k_s, v_s, q_s, sg_s = _proj(
        x1s, past + jnp.arange(ts, dtype=jnp.int32), kv_norm, b_norm[0], kv_w, b_w_in[0], k_norm,
        b_q_norm[0], prompt=False, rows=ts, kt=ts)
    y_s = _sample_attn(q_s, k_s, v_s, cache_k, cache_v, sg_s, x1s, b_w_out[0], b_head_norm[0],
                       *lam_args, lam_init=lam_init)

    return (y_p, y_s,
            conv_state_out(conv_p, nb), h_p[None],
            k_p.reshape(nb, t, nh, V_DIM), v_p.reshape(nb, t, nh, V_DIM),
            conv_state_out(conv_s, nbs), h_s[None],
            k_s.reshape(nbs, ts, nh, V_DIM), v_s.reshape(nbs, ts, nh, V_DIM))
```

```python
import functools
import math

import numpy as np
import jax
import jax.numpy as jnp
from jax import lax
from jax.experimental import pallas as pl
from jax.experimental.pallas import tpu as pltpu

LANE = 128
SUBLANE = 8
VMEM_LIMIT_BYTES = 56 * 1024 * 1024
VMEM_LIMIT_PROJ_BYTES = 58 * 1024 * 1024

CHUNK = 64
CONV_W = 4
LRU_C = 8.0
HEAD_DIM = 64
V_DIM = 2 * HEAD_DIM
ROPE_THETA = 10000.0
EPS = 1e-6
LOG2E = 1.4426950408889634
NEG_BIG = -1e30
BF16_ROWS = 16
KEY_NORM_SLACK = 1.0 + 2.0 ** -6
DENOM_FLOOR = 2.0 ** -100

F32 = jnp.float32
BF16 = jnp.bfloat16


def _gate_band_layout(d_rnn, blk):
    windows = []
    for j in range(d_rnn // LANE):
        lo = ((LANE * j) // blk) * blk
        hi = ((LANE * j + LANE - 1) // blk + 1) * blk
        ws = (lo // LANE) * LANE
        windows.append((ws, -(-(hi - ws) // LANE) * LANE))
    assert all(ws + width <= d_rnn for ws, width in windows)
    return tuple(windows)


def _block_diag_dense(w):
    n, blk, _ = w.shape
    return jnp.concatenate(
        [jnp.pad(w[i], ((0, 0), (blk * i, blk * (n - 1 - i)))) for i in range(n)], axis=0)


def _rnn_layer_kernel(x_ref, conv0_ref, h0_ref, g_ref, w_in_ref, cw_ref, cb_ref, wg_ref, bg_ref,
                      lam_ref, w_out_ref, o_ref, conv_o_ref, h_o_ref,
                      ubuf, a_s, b_s, y_s, hc, *, reset, windows):
    nb, tt, d = x_ref.shape
    rows = nb * tt
    d_rnn = lam_ref.shape[1]
    halo = (CONV_W - 1) * nb
    step = pl.program_id(0)

    @pl.when(step == 0)
    def _():
        ubuf[0:halo, :] = conv0_ref[...]
        hc[...] = h0_ref[...]

    x = jnp.swapaxes(x_ref[...], 0, 1).reshape(rows, d)
    ms = jnp.mean(x * x, axis=-1, keepdims=True)
    xn = (x * lax.rsqrt(ms + EPS) * g_ref[...]).astype(BF16)
    h = jnp.dot(xn, w_in_ref[...], preferred_element_type=F32)
    ubuf[halo:halo + rows, :] = h[:, :d_rnn]
    half_gate = h[:, d_rnn:]

    u = cb_ref[...] + cw_ref[CONV_W - 1:CONV_W, :] * ubuf[halo:halo + rows, :]
    for j in range(CONV_W - 1):
        u = u + cw_ref[j:j + 1, :] * ubuf[j * nb:j * nb + rows, :]
    ubuf[0:halo, :] = ubuf[rows:rows + halo, :]
    conv_o_ref[...] = ubuf[0:halo, :]

    lam = lam_ref[...]
    softplus_neg = jnp.maximum(-lam, 0.0) + jnp.log1p(jnp.exp(-jnp.abs(lam)))
    c2 = (-0.5 * LOG2E * LRU_C) * softplus_neg

    u_bf = u.astype(BF16)
    for j, (ws, width) in enumerate(windows):
        cs = slice(j * LANE, (j + 1) * LANE)
        ri = jnp.dot(u_bf[:, ws:ws + width], wg_ref[j, 0:width, :],
                     preferred_element_type=F32) + bg_ref[j]
        t_r = jnp.tanh(ri[:, :LANE])
        t_i = jnp.tanh(ri[:, LANE:])
        a = jnp.exp2(c2[:, cs] * t_r + c2[:, cs])
        y = 1.0 - a * a
        mult = jnp.where(y > 0.0, y * lax.rsqrt(y), 0.0)
        if reset:
            head = jnp.where(step == 0, 1.0, mult[0:nb])
            mult = jnp.concatenate([head, mult[nb:]], axis=0)
        a_s[:, cs] = a
        b_s[:, cs] = (mult * u[:, cs]) * (0.5 * t_i + 0.5)

    def scan_step(t, hcur):
        r0 = pl.multiple_of(t * nb, SUBLANE)
        hn = a_s[pl.ds(r0, nb), :] * hcur + b_s[pl.ds(r0, nb), :]
        y_s[pl.ds(r0, nb), :] = hn
        return hn

    hlast = lax.fori_loop(0, tt, scan_step, hc[...], unroll=min(tt, 8))
    hc[...] = hlast
    h_o_ref[...] = hlast

    yg = (y_s[...] * (half_gate * (jnp.tanh(half_gate) + 1.0))).astype(BF16)
    out = x + jnp.dot(yg, w_out_ref[...], preferred_element_type=F32)
    o_ref[...] = jnp.swapaxes(out.reshape(tt, nb, d), 0, 1)


def _rnn_layer(x, conv0, h0, norm_g, w_in, conv_w, conv_b, wr, br, wi, bi, lam, w_out, *, reset, tt):
    nb, t, d = x.shape
    assert nb == SUBLANE and t % tt == 0 and tt % SUBLANE == 0
    d_rnn = lam.shape[-1]
    n_blk, blk, _ = wr.shape
    windows = _gate_band_layout(d_rnn, blk)
    gate_k = max(width for _, width in windows)
    ncol = d_rnn // LANE
    rows = nb * tt
    halo = (CONV_W - 1) * nb

    dense_r = _block_diag_dense(wr)
    dense_i = _block_diag_dense(wi)
    wg = jnp.stack([
        jnp.pad(jnp.concatenate([dense_r[ws:ws + width, j * LANE:(j + 1) * LANE],
                                 dense_i[ws:ws + width, j * LANE:(j + 1) * LANE]], axis=1),
                ((0, gate_k - width), (0, 0)))
        for j, (ws, width) in enumerate(windows)])
    bg = jnp.concatenate([br.reshape(ncol, 1, LANE), bi.reshape(ncol, 1, LANE)], axis=2)
    wg = (0.5 * wg).astype(BF16)
    bg = 0.5 * bg
    gate_cols = jnp.arange(2 * d_rnn) >= d_rnn
    w_in = (w_in * jnp.where(gate_cols, 0.5, 1.0)).astype(BF16)

    full = lambda shape: pl.BlockSpec(shape, lambda i: (0,) * len(shape))
    kern = functools.partial(_rnn_layer_kernel, reset=reset, windows=windows)
    return pl.pallas_call(
        kern,
        grid=(t // tt,),
        in_specs=[
            pl.BlockSpec((nb, tt, d), lambda i: (0, i, 0)),
            full((halo, d_rnn)), full((nb, d_rnn)), full((1, d)),
            full((d, 2 * d_rnn)), full((CONV_W, d_rnn)), full((1, d_rnn)),
            full((ncol, gate_k, 2 * LANE)), full((ncol, 1, 2 * LANE)), full((1, d_rnn)),
            full((d_rnn, d)),
        ],
        out_specs=[
            pl.BlockSpec((nb, tt, d), lambda i: (0, i, 0)),
            full((halo, d_rnn)), full((nb, d_rnn)),
        ],
        out_shape=[
            jax.ShapeDtypeStruct((nb, t, d), F32),
            jax.ShapeDtypeStruct((halo, d_rnn), F32),
            jax.ShapeDtypeStruct((nb, d_rnn), F32),
        ],
        scratch_shapes=[
            pltpu.VMEM((halo + rows, d_rnn), F32),
            pltpu.VMEM((rows, d_rnn), F32),
            pltpu.VMEM((rows, d_rnn), F32),
            pltpu.VMEM((rows, d_rnn), F32),
            pltpu.VMEM((nb, d_rnn), F32),
        ],
        compiler_params=pltpu.CompilerParams(
            dimension_semantics=("arbitrary",), vmem_limit_bytes=VMEM_LIMIT_BYTES),
        name="rnn_layer_reset" if reset else "rnn_layer_state",
    )(x, conv0, h0, norm_g.reshape(1, d), w_in, conv_w, conv_b.reshape(1, d_rnn),
      wg, bg, lam.reshape(1, d_rnn), w_out.astype(BF16))


def _group_sum_matrices(width):
    groups = width // HEAD_DIM
    assert groups <= LANE
    gather = np.zeros((width, LANE), np.float32)
    gather[np.arange(width), np.arange(width) // HEAD_DIM] = 1.0
    expand = np.concatenate([gather.T, gather.T], axis=0)
    return jnp.asarray(gather, BF16), jnp.asarray(expand, BF16)


def _group_rms_scale(x, gather, expand):
    ssum = jnp.dot((x * x).astype(BF16), gather, preferred_element_type=F32)
    inv = lax.rsqrt(ssum * (1.0 / HEAD_DIM) + EPS)
    hi = inv.astype(BF16)
    lo = (inv - hi.astype(F32)).astype(BF16)
    return jnp.dot(jnp.concatenate([hi, lo], axis=1), expand, preferred_element_type=F32)


def _rope_slab(xh, cos_t, sin_t):
    lane = lax.broadcasted_iota(jnp.int32, xh.shape, 1)
    first_half = (lane % HEAD_DIM) < (HEAD_DIM // 2)
    partner = jnp.where(first_half, pltpu.roll(xh, LANE - HEAD_DIM // 2, 1),
                        pltpu.roll(xh, HEAD_DIM // 2, 1))
    return xh * cos_t + partner * sin_t


def _proj_kernel(x_ref, gkv_ref, gb_ref, kvw_ref, bw_ref, gk_ref, gq_ref, cos_ref, sin_ref,
                 gather_ref, expand_ref, *out_refs, prompt, kt):
    if prompt:
        k_ref, v_ref, khm_ref, vt_ref, qt_ref, sg_ref = out_refs
    else:
        k_ref, v_ref, q_ref, sg_ref = out_refs
    rows = x_ref.shape[1]
    w = gk_ref.shape[1]
    nh = w // V_DIM
    sub = kt if prompt else rows
    gather, expand = gather_ref[...], expand_ref[...]
    if prompt:
        pad_row = lax.broadcasted_iota(jnp.int32, (BF16_ROWS, kt), 0)
        ones_row = jnp.where(pad_row == 0, 1.0, 0.0).astype(BF16)
    for c in range(rows // sub):
        rs = slice(c * sub, (c + 1) * sub)
        x = x_ref[0, rs, :]
        ms = jnp.mean(x * x, axis=-1, keepdims=True)
        xh = x * lax.rsqrt(ms + EPS)
        kv = jnp.dot((xh * gkv_ref[...]).astype(BF16), kvw_ref[...], preferred_element_type=F32)
        qg = jnp.dot((xh * gb_ref[...]).astype(BF16), bw_ref[...], preferred_element_type=F32)
        k_raw, v = kv[:, :w], kv[:, w:]
        q_raw, half_gate = qg[:, :w], qg[:, w:]
        kn = k_raw * _group_rms_scale(k_raw, gather, expand) * gk_ref[...]
        qn = q_raw * _group_rms_scale(q_raw, gather, expand) * gq_ref[...]
        sg = half_gate * (jnp.tanh(half_gate) + 1.0)
        cos_t, sin_t = cos_ref[rs, :], sin_ref[rs, :]
        if not prompt:
            sg_ref[0] = sg
        for h in range(nh):
            cs = slice(h * V_DIM, (h + 1) * V_DIM)
            kh = _rope_slab(kn[:, cs], cos_t, sin_t)
            qh = _rope_slab(qn[:, cs], cos_t, sin_t)
            k_ref[0, pl.ds(c * sub * nh + h, sub, stride=nh), :] = kh
            v_ref[0, pl.ds(c * sub * nh + h, sub, stride=nh), :] = v[:, cs]
            if prompt:
                khm_ref[0, h, rs, :] = kh.astype(BF16)
                sg_ref[0, h, rs, :] = sg[:, cs].astype(BF16)
                vt_ref[0, h, c, 0:V_DIM, :] = v[:, cs].T.astype(BF16)
                vt_ref[0, h, c, V_DIM:V_DIM + BF16_ROWS, :] = ones_row
                qt_ref[0, h, c] = qh.T.astype(BF16)
            else:
                q_ref[0, :, cs] = qh


def _rope_tables(pos):
    half = HEAD_DIM // 2
    inv = ROPE_THETA ** (-jnp.arange(half, dtype=F32) / half)
    ang = pos.astype(F32)[:, None] * inv[None, :]
    cos, sin = jnp.cos(ang), jnp.sin(ang)
    reps = LANE // HEAD_DIM
    return (jnp.tile(jnp.concatenate([cos, cos], axis=1), (1, reps)),
            jnp.tile(jnp.concatenate([-sin, sin], axis=1), (1, reps)))


def _proj(x, pos, kv_norm, b_norm, kv_w, b_w_in, k_norm, q_norm, *, prompt, rows, kt):
    nb, t, d = x.shape
    w = kv_w.shape[1] // 2
    nh = w // V_DIM
    assert t % rows == 0 and (not prompt or rows % kt == 0)
    cos_t, sin_t = _rope_tables(pos)
    gather, expand = _group_sum_matrices(w)
    reps = w // HEAD_DIM
    gk = jnp.tile(k_norm.astype(F32), reps).reshape(1, w)
    gq = (jnp.tile(q_norm.astype(F32), reps) * (HEAD_DIM ** -0.5 * LOG2E)).reshape(1, w)
    b_w_half = (b_w_in * jnp.where(jnp.arange(2 * w) >= w, 0.5, 1.0)).astype(BF16)

    full = lambda shape: pl.BlockSpec(shape, lambda b, i: (0,) * len(shape),
                                      pipeline_mode=pl.Buffered(1))
    tok = pl.BlockSpec((1, rows, w), lambda b, i: (b, i, 0))
    tok_heads = pl.BlockSpec((1, rows * nh, V_DIM), lambda b, i: (b, i, 0))
    out_specs = [tok_heads, tok_heads]
    out_shape = [jax.ShapeDtypeStruct((nb, t * nh, V_DIM), F32)] * 2
    if prompt:
        nc = rows // kt
        hm = pl.BlockSpec((1, nh, rows, V_DIM), lambda b, i: (b, 0, i, 0))
        tr = lambda r: pl.BlockSpec((1, nh, nc, r, kt), lambda b, i: (b, 0, i, 0, 0))
        out_specs += [hm, tr(V_DIM + BF16_ROWS), tr(V_DIM), hm]
        out_shape += [jax.ShapeDtypeStruct((nb, nh, t, V_DIM), BF16),
                      jax.ShapeDtypeStruct((nb, nh, t // kt, V_DIM + BF16_ROWS, kt), BF16),
                      jax.ShapeDtypeStruct((nb, nh, t // kt, V_DIM, kt), BF16),
                      jax.ShapeDtypeStruct((nb, nh, t, V_DIM), BF16)]
    else:
        out_specs += [tok, tok]
        out_shape += [jax.ShapeDtypeStruct((nb, t, w), F32)] * 2
    return pl.pallas_call(
        functools.partial(_proj_kernel, prompt=prompt, kt=kt),
        grid=(nb, t // rows),
        in_specs=[
            pl.BlockSpec((1, rows, d), lambda b, i: (b, i, 0)),
            full((1, d)), full((1, d)), full((d, 2 * w)), full((d, 2 * w)),
            full((1, w)), full((1, w)),
            pl.BlockSpec((rows, LANE), lambda b, i: (i, 0)),
            pl.BlockSpec((rows, LANE), lambda b, i: (i, 0)),
            full((w, LANE)), full((2 * LANE, w)),
        ],
        out_specs=out_specs,
        out_shape=out_shape,
        compiler_params=pltpu.CompilerParams(
            dimension_semantics=("parallel", "parallel"), vmem_limit_bytes=VMEM_LIMIT_PROJ_BYTES),
        name="proj_prompt" if prompt else "proj_sample",
    )(x, kv_norm.reshape(1, d), b_norm.reshape(1, d), kv_w.astype(BF16), b_w_half,
      gk, gq, cos_t, sin_t, gather, expand)


def _diff_lambda(lq1, lk1, lq2, lk2, lam_init):
    return (jnp.exp(jnp.sum(lq1 * lk1, axis=-1, keepdims=True))
            - jnp.exp(jnp.sum(lq2 * lk2, axis=-1, keepdims=True)) + lam_init)


def _prompt_attn_kernel(qt_ref, khm_ref, vt_ref, sg_ref, x_ref, w_out_ref, hg_ref,
                        lq1_ref, lk1_ref, lq2_ref, lk2_ref, gk_ref, o_ref,
                        qm_s, shift_s, acc_s, p_s, oh_s, *, lam_init):
    nh, vd, tq = qt_ref.shape[1], qt_ref.shape[3], qt_ref.shape[4]
    qi = pl.program_id(1)
    chains = [(h, c) for h in range(nh) for c in range(2)]

    key_norm_bound = (math.sqrt(HEAD_DIM) * KEY_NORM_SLACK
                      * jnp.max(jnp.abs(gk_ref[...]), axis=-1, keepdims=True))
    feat = lax.broadcasted_iota(jnp.int32, (vd, tq), 0)
    for h in range(nh):
        qt = qt_ref[0, h, 0]
        zero = jnp.zeros_like(qt)
        for c in range(2):
            qm = jnp.where((feat >= HEAD_DIM) if c else (feat < HEAD_DIM), qt, zero)
            qm_s[h, c] = qm
            qf = qm.astype(F32)
            shift_s[h, c] = jnp.sqrt(jnp.sum(qf * qf, axis=0, keepdims=True)) * key_norm_bound

    def visible_mask():
        krow = lax.broadcasted_iota(jnp.int32, (tq, tq), 0)
        qcol = lax.broadcasted_iota(jnp.int32, (tq, tq), 1)
        return (krow // CHUNK) <= (qcol // CHUNK)

    def key_tile(h, kt):
        return khm_ref[0, h, pl.ds(pl.multiple_of(kt * tq, tq), tq), :]

    def epilogue():
        lam = _diff_lambda(lq1_ref[...], lk1_ref[...], lq2_ref[...], lk2_ref[...], lam_init)
        for h in range(nh):
            w1 = 1.0 / acc_s[h, 0, vd:vd + 1, :]
            w2 = lam / acc_s[h, 1, vd:vd + 1, :]
            ot = acc_s[h, 0, 0:vd, :] * w1 - acc_s[h, 1, 0:vd, :] * w2
            ms = jnp.mean(ot * ot, axis=0, keepdims=True)
            on = (ot * lax.rsqrt(ms + EPS)) * hg_ref[...]
            oh_s[:, h * vd:(h + 1) * vd] = (on.T * sg_ref[0, h].astype(F32)).astype(BF16)
        o_ref[0] = x_ref[0] + jnp.dot(oh_s[...], w_out_ref[...], preferred_element_type=F32)

    def tile_update(kt, masked):
        if masked:
            visible = visible_mask()
        for h, c in chains:
            s = jnp.dot(key_tile(h, kt), qm_s[h, c], preferred_element_type=F32)
            p = jnp.exp2(s - shift_s[h, c])
            if masked:
                p = jnp.where(visible, p, 0.0)
            p_s[h, c] = p.astype(BF16)
        for h, c in chains:
            acc_s[h, c] += jnp.dot(vt_ref[0, h, kt], p_s[h, c], preferred_element_type=F32)

    def pair_body(i, carry):
        tile_update(2 * i, False)
        tile_update(2 * i + 1, False)
        return carry

    acc_s[...] = jnp.zeros(acc_s.shape, F32)
    lax.fori_loop(0, qi // 2, pair_body, 0)

    @pl.when(qi % 2 == 1)
    def _():
        tile_update(qi - 1, False)

    tile_update(qi, True)
    epilogue()

    denom_min = jnp.min(acc_s[:, :, vd:vd + 1, :])

    @pl.when(jnp.logical_not(denom_min >= DENOM_FLOOR))
    def _():
        visible = visible_mask()

        def head_body(h, carry):
            def tile_step(kt, state, masked):
                new = []
                for c, (m, acc) in enumerate(state):
                    s = jnp.dot(key_tile(h, kt), qm_s[h, c], preferred_element_type=F32)
                    if masked:
                        s = jnp.where(visible, s, NEG_BIG)
                    m_new = jnp.maximum(m, jnp.max(s, axis=0, keepdims=True))
                    p = jnp.exp2(s - m_new).astype(BF16)
                    acc_new = jnp.exp2(m - m_new) * acc + jnp.dot(
                        vt_ref[0, h, kt], p, preferred_element_type=F32)
                    new.append((m_new, acc_new))
                return tuple(new)

            init = tuple((jnp.full((1, tq), NEG_BIG, F32), jnp.zeros(acc_s.shape[2:], F32))
                         for _ in range(2))
            state = lax.fori_loop(0, qi, lambda kt, st: tile_step(kt, st, False), init)
            state = tile_step(qi, state, True)
            for c in range(2):
                acc_s[h, c] = state[c][1]
            return carry

        lax.fori_loop(0, nh, head_body, 0)
        epilogue()


def _prompt_attn(qt, khm, vt, sg, x, w_out, head_g, lq1, lk1, lq2, lk2, k_norm, *, lam_init):
    nb, nh, nq, vd, tq = qt.shape
    vde = vt.shape[3]
    t = khm.shape[2]
    d = x.shape[-1]
    w = nh * vd
    hg = jnp.broadcast_to((head_g.astype(F32) * (1.0 - lam_init))[:, None], (vd, tq))
    full = lambda shape: pl.BlockSpec(shape, lambda b, i: (0,) * len(shape))
    row = lambda a: a.reshape(1, -1).astype(F32)
    return pl.pallas_call(
        functools.partial(_prompt_attn_kernel, lam_init=lam_init),
        grid=(nb, nq),
        in_specs=[
            pl.BlockSpec((1, nh, 1, vd, tq), lambda b, i: (b, 0, i, 0, 0)),
            pl.BlockSpec((1, nh, t, vd), lambda b, i: (b, 0, 0, 0)),
            pl.BlockSpec((1, nh, nq, vde, tq), lambda b, i: (b, 0, 0, 0, 0)),
            pl.BlockSpec((1, nh, tq, vd), lambda b, i: (b, 0, i, 0)),
            pl.BlockSpec((1, tq, d), lambda b, i: (b, i, 0)),
            full((w, d)), full((vd, tq)),
            full((1, HEAD_DIM)), full((1, HEAD_DIM)), full((1, HEAD_DIM)), full((1, HEAD_DIM)),
            full((1, HEAD_DIM)),
        ],
        out_specs=pl.BlockSpec((1, tq, d), lambda b, i: (b, i, 0)),
        out_shape=jax.ShapeDtypeStruct((nb, t, d), F32),
        scratch_shapes=[
            pltpu.VMEM((nh, 2, vd, tq), BF16),
            pltpu.VMEM((nh, 2, 1, tq), F32),
            pltpu.VMEM((nh, 2, vde, tq), F32),
            pltpu.VMEM((nh, 2, tq, tq), BF16),
            pltpu.VMEM((tq, w), BF16),
        ],
        compiler_params=pltpu.CompilerParams(
            dimension_semantics=("parallel", "arbitrary"), vmem_limit_bytes=VMEM_LIMIT_BYTES),
        name="attn_prompt",
    )(qt, khm, vt, sg, x, w_out.astype(BF16), hg, row(lq1), row(lk1), row(lq2), row(lk2),
      row(k_norm))


def _sample_attn_kernel(q_ref, kn_ref, vn_ref, ck_ref, cv_ref, sg_ref, x_ref, w_out_ref, hg_ref,
                        lq1_ref, lk1_ref, lq2_ref, lk2_ref, o_ref, *, lam_init):
    q = q_ref[0]
    tq, w = q.shape
    nh = w // V_DIM
    past = ck_ref.shape[1] // nh
    lam = _diff_lambda(lq1_ref[...], lk1_ref[...], lq2_ref[...], lk2_ref[...], lam_init)
    lane = lax.broadcasted_iota(jnp.int32, (tq, V_DIM), 1)
    contract_last = (((1,), (1,)), ((), ()))
    outs = []
    for h in range(nh):
        cs = slice(h * V_DIM, (h + 1) * V_DIM)
        qh = q[:, cs]
        ck = ck_ref[0, pl.ds(h, past, stride=nh), :].astype(BF16)
        cv = cv_ref[0, pl.ds(h, past, stride=nh), :].astype(BF16)
        kn = kn_ref[0, pl.ds(h, tq, stride=nh), :].astype(BF16)
        vn = vn_ref[0, pl.ds(h, tq, stride=nh), :].astype(BF16)
        q2 = jnp.concatenate([jnp.where(lane < HEAD_DIM, qh, 0.0),
                              jnp.where(lane >= HEAD_DIM, qh, 0.0)], axis=0).astype(BF16)
        s_old = lax.dot_general(q2, ck, contract_last, preferred_element_type=F32)
        s_new = lax.dot_general(q2, kn, contract_last, preferred_element_type=F32)
        m = jnp.maximum(jnp.max(s_old, axis=-1, keepdims=True),
                        jnp.max(s_new, axis=-1, keepdims=True))
        p_old = jnp.exp2(s_old - m)
        p_new = jnp.exp2(s_new - m)
        l = jnp.sum(p_old, axis=-1, keepdims=True) + jnp.sum(p_new, axis=-1, keepdims=True)
        pv = (jnp.dot(p_old.astype(BF16), cv, preferred_element_type=F32)
              + jnp.dot(p_new.astype(BF16), vn, preferred_element_type=F32)) / l
        oh = pv[0:tq] - lam * pv[tq:2 * tq]
        ms = jnp.mean(oh * oh, axis=-1, keepdims=True)
        outs.append(oh * lax.rsqrt(ms + EPS) * hg_ref[...] * (1.0 - lam_init))
    o = (jnp.concatenate(outs, axis=1) * sg_ref[0]).astype(BF16)
    o_ref[0] = x_ref[0] + jnp.dot(o, w_out_ref[...], preferred_element_type=F32)


def _sample_attn(q, k_new, v_new, cache_k, cache_v, sg, x, w_out, head_g, lq1, lk1, lq2, lk2, *, lam_init):
    nb, tq, w = q.shape
    past = cache_k.shape[1]
    d = x.shape[-1]
    full = lambda shape: pl.BlockSpec(shape, lambda b: (0,) * len(shape))
    tok = lambda width: pl.BlockSpec((1, tq, width), lambda b: (b, 0, 0))
    nh = w // V_DIM
    old = pl.BlockSpec((1, past * nh, V_DIM), lambda b: (b, 0, 0))
    new = pl.BlockSpec((1, tq * nh, V_DIM), lambda b: (b, 0, 0))
    row = lambda a: a.reshape(1, -1).astype(F32)
    return pl.pallas_call(
        functools.partial(_sample_attn_kernel, lam_init=lam_init),
        grid=(nb,),
        in_specs=[tok(w), new, new, old, old, tok(w), tok(d),
                  full((w, d)), full((1, V_DIM)),
                  full((1, HEAD_DIM)), full((1, HEAD_DIM)), full((1, HEAD_DIM)), full((1, HEAD_DIM))],
        out_specs=tok(d),
        out_shape=jax.ShapeDtypeStruct((nb, tq, d), F32),
        compiler_params=pltpu.CompilerParams(
            dimension_semantics=("parallel",), vmem_limit_bytes=VMEM_LIMIT_BYTES),
        name="attn_sample",
    )(q, k_new, v_new, cache_k.reshape(nb, past * nh, V_DIM), cache_v.reshape(nb, past * nh, V_DIM), sg, x,
      w_out.astype(BF16), row(head_g), row(lq1), row(lk1), row(lq2), row(lk2))


def _pick_tile(n, target):
    tile = min(n, target)
    while n % tile:
        tile //= 2
    return tile


def kernel(x_prompt, x_sample, state_conv, state_h, cache_k, cache_v, a_norm, a_w_in, a_conv_w, a_conv_b, a_gate_r_w, a_gate_r_b, a_gate_i_w, a_gate_i_b, a_lambda, a_w_out, kv_norm, kv_w, k_norm, b_norm, b_w_in, b_q_norm, b_lambda_q1, b_lambda_k1, b_lambda_q2, b_lambda_k2, b_head_norm, b_w_out):
    n_a, n_b = a_norm.shape[0], b_norm.shape[0]
    assert n_a == 1 and n_b == 1, "one recurrent layer followed by one attention layer"
    nb, t, d = x_prompt.shape
    nbs, ts, _ = x_sample.shape
    d_rnn = a_lambda.shape[-1]
    w = kv_w.shape[1] // 2
    nh = w // V_DIM
    past = cache_k.shape[1]
    halo = CONV_W - 1
    lam_init = 0.8 - 0.6 * math.exp(-0.3 * n_a)
    a_args = (a_norm[0], a_w_in[0], a_conv_w[0], a_conv_b[0], a_gate_r_w[0], a_gate_r_b[0],
              a_gate_i_w[0], a_gate_i_b[0], a_lambda[0], a_w_out[0])
    lam_args = (b_lambda_q1[0], b_lambda_k1[0], b_lambda_q2[0], b_lambda_k2[0])

    def conv_state_out(c, n):
        return jnp.swapaxes(c.reshape(halo, n, d_rnn), 0, 1)[None]

    x1, conv_p, h_p = _rnn_layer(
        x_prompt, jnp.zeros((halo * nb, d_rnn), F32), jnp.zeros((nb, d_rnn), F32), *a_args,
        reset=True, tt=_pick_tile(t, 64))
    tq = _pick_tile(t, 256)
    k_p, v_p, khm, vt, qt, sg = _proj(
        x1, jnp.arange(t, dtype=jnp.int32), kv_norm, b_norm[0], kv_w, b_w_in[0], k_norm, b_q_norm[0],
        prompt=True, rows=_pick_tile(t, 1024), kt=tq)
    y_p = _prompt_attn(qt, khm, vt, sg, x1, b_w_out[0], b_head_norm[0], *lam_args, k_norm,
                       lam_init=lam_init)

    conv0 = jnp.swapaxes(state_conv[0], 0, 1).reshape(halo * nbs, d_rnn)
    x1s, conv_s, h_s = _rnn_layer(x_sample, conv0, state_h[0], *a_args, reset=False, tt=ts)
    k_s, v_s, q_s, sg_s = _proj(
        x1s.reshape(1, nbs * ts, d), jnp.tile(past + jnp.arange(ts, dtype=jnp.int32), nbs),
        kv_norm, b_norm[0], kv_w, b_w_in[0], k_norm, b_q_norm[0], prompt=False, rows=nbs * ts, kt=ts)
    k_s, v_s = k_s.reshape(nbs, ts * nh, V_DIM), v_s.reshape(nbs, ts * nh, V_DIM)
    q_s, sg_s = q_s.reshape(nbs, ts, w), sg_s.reshape(nbs, ts, w)
    y_s = _sample_attn(q_s, k_s, v_s, cache_k, cache_v, sg_s, x1s, b_w_out[0], b_head_norm[0],
                       *lam_args, lam_init=lam_init)

    return (y_p, y_s,
            conv_state_out(conv_p, nb), h_p[None],
            k_p.reshape(nb, t, nh, V_DIM), v_p.reshape(nb, t, nh, V_DIM),
            conv_state_out(conv_s, nbs), h_s[None],
            k_s.reshape(nbs, ts, nh, V_DIM), v_s.reshape(nbs, ts, nh, V_DIM))
```

```python
import functools
import math

import numpy as np
import jax
import jax.numpy as jnp
from jax import lax
from jax.experimental import pallas as pl
from jax.experimental.pallas import tpu as pltpu

LANE = 128
SUBLANE = 8
VMEM_LIMIT_BYTES = 56 * 1024 * 1024
VMEM_LIMIT_PROJ_BYTES = 58 * 1024 * 1024

CHUNK = 64
CONV_W = 4
LRU_C = 8.0
HEAD_DIM = 64
V_DIM = 2 * HEAD_DIM
ROPE_THETA = 10000.0
EPS = 1e-6
LOG2E = 1.4426950408889634
NEG_BIG = -1e30
BF16_ROWS = 16
KEY_NORM_SLACK = 1.0 + 2.0 ** -6
DENOM_FLOOR = 2.0 ** -100

F32 = jnp.float32
BF16 = jnp.bfloat16


def _gate_band_layout(d_rnn, blk):
    windows = []
    for j in range(d_rnn // LANE):
        lo = ((LANE * j) // blk) * blk
        hi = ((LANE * j + LANE - 1) // blk + 1) * blk
        ws = (lo // LANE) * LANE
        windows.append((ws, -(-(hi - ws) // LANE) * LANE))
    assert all(ws + width <= d_rnn for ws, width in windows)
    return tuple(windows)


def _block_diag_dense(w):
    n, blk, _ = w.shape
    return jnp.concatenate(
        [jnp.pad(w[i], ((0, 0), (blk * i, blk * (n - 1 - i)))) for i in range(n)], axis=0)


def _rnn_layer_kernel(x_ref, conv0_ref, h0_ref, g_ref, w_in_ref, cw_ref, cb_ref, wg_ref, bg_ref,
                      lam_ref, w_out_ref, o_ref, conv_o_ref, h_o_ref,
                      ubuf, a_s, b_s, y_s, hc, *, reset, windows):
    nb, tt, d = x_ref.shape
    rows = nb * tt
    d_rnn = lam_ref.shape[1]
    halo = (CONV_W - 1) * nb
    step = pl.program_id(0)

    @pl.when(step == 0)
    def _():
        ubuf[0:halo, :] = conv0_ref[...]
        hc[...] = h0_ref[...]

    x = jnp.swapaxes(x_ref[...], 0, 1).reshape(rows, d)
    ms = jnp.mean(x * x, axis=-1, keepdims=True)
    xn = (x * lax.rsqrt(ms + EPS) * g_ref[...]).astype(BF16)
    h = jnp.dot(xn, w_in_ref[...], preferred_element_type=F32)
    ubuf[halo:halo + rows, :] = h[:, :d_rnn]
    half_gate = h[:, d_rnn:]

    u = cb_ref[...] + cw_ref[CONV_W - 1:CONV_W, :] * ubuf[halo:halo + rows, :]
    for j in range(CONV_W - 1):
        u = u + cw_ref[j:j + 1, :] * ubuf[j * nb:j * nb + rows, :]
    ubuf[0:halo, :] = ubuf[rows:rows + halo, :]
    conv_o_ref[...] = ubuf[0:halo, :]

    lam = lam_ref[...]
    softplus_neg = jnp.maximum(-lam, 0.0) + jnp.log1p(jnp.exp(-jnp.abs(lam)))
    c2 = (-0.5 * LOG2E * LRU_C) * softplus_neg

    u_bf = u.astype(BF16)
    for j, (ws, width) in enumerate(windows):
        cs = slice(j * LANE, (j + 1) * LANE)
        ri = jnp.dot(u_bf[:, ws:ws + width], wg_ref[j, 0:width, :],
                     preferred_element_type=F32) + bg_ref[j]
        t_r = jnp.tanh(ri[:, :LANE])
        t_i = jnp.tanh(ri[:, LANE:])
        a = jnp.exp2(c2[:, cs] * t_r + c2[:, cs])
        y = 1.0 - a * a
        mult = jnp.where(y > 0.0, y * lax.rsqrt(y), 0.0)
        if reset:
            head = jnp.where(step == 0, 1.0, mult[0:nb])
            mult = jnp.concatenate([head, mult[nb:]], axis=0)
        a_s[:, cs] = a
        b_s[:, cs] = (mult * u[:, cs]) * (0.5 * t_i + 0.5)

    def scan_step(t, hcur):
        r0 = pl.multiple_of(t * nb, SUBLANE)
        hn = a_s[pl.ds(r0, nb), :] * hcur + b_s[pl.ds(r0, nb), :]
        y_s[pl.ds(r0, nb), :] = hn
        return hn

    hlast = lax.fori_loop(0, tt, scan_step, hc[...], unroll=min(tt, 8))
    hc[...] = hlast
    h_o_ref[...] = hlast

    yg = (y_s[...] * (half_gate * (jnp.tanh(half_gate) + 1.0))).astype(BF16)
    out = x + jnp.dot(yg, w_out_ref[...], preferred_element_type=F32)
    o_ref[...] = jnp.swapaxes(out.reshape(tt, nb, d), 0, 1)


def _rnn_layer(x, conv0, h0, norm_g, w_in, conv_w, conv_b, wr, br, wi, bi, lam, w_out, *, reset, tt):
    nb, t, d = x.shape
    assert nb == SUBLANE and t % tt == 0 and tt % SUBLANE == 0
    d_rnn = lam.shape[-1]
    n_blk, blk, _ = wr.shape
    windows = _gate_band_layout(d_rnn, blk)
    gate_k = max(width for _, width in windows)
    ncol = d_rnn // LANE
    rows = nb * tt
    halo = (CONV_W - 1) * nb

    dense_r = _block_diag_dense(wr)
    dense_i = _block_diag_dense(wi)
    wg = jnp.stack([
        jnp.pad(jnp.concatenate([dense_r[ws:ws + width, j * LANE:(j + 1) * LANE],
                                 dense_i[ws:ws + width, j * LANE:(j + 1) * LANE]], axis=1),
                ((0, gate_k - width), (0, 0)))
        for j, (ws, width) in enumerate(windows)])
    bg = jnp.concatenate([br.reshape(ncol, 1, LANE), bi.reshape(ncol, 1, LANE)], axis=2)
    wg = (0.5 * wg).astype(BF16)
    bg = 0.5 * bg
    gate_cols = jnp.arange(2 * d_rnn) >= d_rnn
    w_in = (w_in * jnp.where(gate_cols, 0.5, 1.0)).astype(BF16)

    full = lambda shape: pl.BlockSpec(shape, lambda i: (0,) * len(shape))
    kern = functools.partial(_rnn_layer_kernel, reset=reset, windows=windows)
    return pl.pallas_call(
        kern,
        grid=(t // tt,),
        in_specs=[
            pl.BlockSpec((nb, tt, d), lambda i: (0, i, 0)),
            full((halo, d_rnn)), full((nb, d_rnn)), full((1, d)),
            full((d, 2 * d_rnn)), full((CONV_W, d_rnn)), full((1, d_rnn)),
            full((ncol, gate_k, 2 * LANE)), full((ncol, 1, 2 * LANE)), full((1, d_rnn)),
            full((d_rnn, d)),
        ],
        out_specs=[
            pl.BlockSpec((nb, tt, d), lambda i: (0, i, 0)),
            full((halo, d_rnn)), full((nb, d_rnn)),
        ],
        out_shape=[
            jax.ShapeDtypeStruct((nb, t, d), F32),
            jax.ShapeDtypeStruct((halo, d_rnn), F32),
            jax.ShapeDtypeStruct((nb, d_rnn), F32),
        ],
        scratch_shapes=[
            pltpu.VMEM((halo + rows, d_rnn), F32),
            pltpu.VMEM((rows, d_rnn), F32),
            pltpu.VMEM((rows, d_rnn), F32),
            pltpu.VMEM((rows, d_rnn), F32),
            pltpu.VMEM((nb, d_rnn), F32),
        ],
        compiler_params=pltpu.CompilerParams(
            dimension_semantics=("arbitrary",), vmem_limit_bytes=VMEM_LIMIT_BYTES),
        name="rnn_layer_reset" if reset else "rnn_layer_state",
    )(x, conv0, h0, norm_g.reshape(1, d), w_in, conv_w, conv_b.reshape(1, d_rnn),
      wg, bg, lam.reshape(1, d_rnn), w_out.astype(BF16))


def _group_sum_matrices(width):
    groups = width // HEAD_DIM
    assert groups <= LANE
    gather = np.zeros((width, LANE), np.float32)
    gather[np.arange(width), np.arange(width) // HEAD_DIM] = 1.0
    expand = np.concatenate([gather.T, gather.T], axis=0)
    return jnp.asarray(gather, BF16), jnp.asarray(expand, BF16)


def _group_rms_scale(x, gather, expand):
    ssum = jnp.dot((x * x).astype(BF16), gather, preferred_element_type=F32)
    inv = lax.rsqrt(ssum * (1.0 / HEAD_DIM) + EPS)
    hi = inv.astype(BF16)
    lo = (inv - hi.astype(F32)).astype(BF16)
    return jnp.dot(jnp.concatenate([hi, lo], axis=1), expand, preferred_element_type=F32)


def _rope_slab(xh, cos_t, sin_t):
    lane = lax.broadcasted_iota(jnp.int32, xh.shape, 1)
    first_half = (lane % HEAD_DIM) < (HEAD_DIM // 2)
    partner = jnp.where(first_half, pltpu.roll(xh, LANE - HEAD_DIM // 2, 1),
                        pltpu.roll(xh, HEAD_DIM // 2, 1))
    return xh * cos_t + partner * sin_t


def _proj_kernel(x_ref, gkv_ref, gb_ref, kvw_ref, bw_ref, gk_ref, gq_ref, cos_ref, sin_ref,
                 gather_ref, expand_ref, *out_refs, prompt, kt):
    if prompt:
        k_ref, v_ref, khm_ref, vt_ref, qt_ref, sg_ref = out_refs
    else:
        k_ref, v_ref, q_ref, sg_ref = out_refs
    rows = x_ref.shape[1]
    w = gk_ref.shape[1]
    nh = w // V_DIM
    sub = kt if prompt else rows
    gather, expand = gather_ref[...], expand_ref[...]
    if prompt:
        pad_row = lax.broadcasted_iota(jnp.int32, (BF16_ROWS, kt), 0)
        ones_row = jnp.where(pad_row == 0, 1.0, 0.0).astype(BF16)
    for c in range(rows // sub):
        rs = slice(c * sub, (c + 1) * sub)
        x = x_ref[0, rs, :]
        ms = jnp.mean(x * x, axis=-1, keepdims=True)
        xh = x * lax.rsqrt(ms + EPS)
        kv = jnp.dot((xh * gkv_ref[...]).astype(BF16), kvw_ref[...], preferred_element_type=F32)
        qg = jnp.dot((xh * gb_ref[...]).astype(BF16), bw_ref[...], preferred_element_type=F32)
        k_raw, v = kv[:, :w], kv[:, w:]
        q_raw, half_gate = qg[:, :w], qg[:, w:]
        kn = k_raw * _group_rms_scale(k_raw, gather, expand) * gk_ref[...]
        qn = q_raw * _group_rms_scale(q_raw, gather, expand) * gq_ref[...]
        sg = half_gate * (jnp.tanh(half_gate) + 1.0)
        cos_t, sin_t = cos_ref[rs, :], sin_ref[rs, :]
        if not prompt:
            sg_ref[0] = sg
        for h in range(nh):
            cs = slice(h * V_DIM, (h + 1) * V_DIM)
            kh = _rope_slab(kn[:, cs], cos_t, sin_t)
            qh = _rope_slab(qn[:, cs], cos_t, sin_t)
            k_ref[0, pl.ds(c * sub * nh + h, sub, stride=nh), :] = kh
            v_ref[0, pl.ds(c * sub * nh + h, sub, stride=nh), :] = v[:, cs]
            if prompt:
                khm_ref[0, h, rs, :] = kh.astype(BF16)
                sg_ref[0, h, rs, :] = sg[:, cs].astype(BF16)
                vt_ref[0, h, c, 0:V_DIM, :] = v[:, cs].T.astype(BF16)
                vt_ref[0, h, c, V_DIM:V_DIM + BF16_ROWS, :] = ones_row
                qt_ref[0, h, c] = qh.T.astype(BF16)
            else:
                q_ref[0, :, cs] = qh


def _rope_tables(pos):
    half = HEAD_DIM // 2
    inv = ROPE_THETA ** (-jnp.arange(half, dtype=F32) / half)
    ang = pos.astype(F32)[:, None] * inv[None, :]
    cos, sin = jnp.cos(ang), jnp.sin(ang)
    reps = LANE // HEAD_DIM
    return (jnp.tile(jnp.concatenate([cos, cos], axis=1), (1, reps)),
            jnp.tile(jnp.concatenate([-sin, sin], axis=1), (1, reps)))


def _proj(x, pos, kv_norm, b_norm, kv_w, b_w_in, k_norm, q_norm, *, prompt, rows, kt):
    nb, t, d = x.shape
    w = kv_w.shape[1] // 2
    nh = w // V_DIM
    assert t % rows == 0 and (not prompt or rows % kt == 0)
    cos_t, sin_t = _rope_tables(pos)
    gather, expand = _group_sum_matrices(w)
    reps = w // HEAD_DIM
    gk = jnp.tile(k_norm.astype(F32), reps).reshape(1, w)
    gq = (jnp.tile(q_norm.astype(F32), reps) * (HEAD_DIM ** -0.5 * LOG2E)).reshape(1, w)
    b_w_half = (b_w_in * jnp.where(jnp.arange(2 * w) >= w, 0.5, 1.0)).astype(BF16)

    full = lambda shape: pl.BlockSpec(shape, lambda b, i: (0,) * len(shape),
                                      pipeline_mode=pl.Buffered(1))
    tok = pl.BlockSpec((1, rows, w), lambda b, i: (b, i, 0))
    tok_heads = pl.BlockSpec((1, rows * nh, V_DIM), lambda b, i: (b, i, 0))
    out_specs = [tok_heads, tok_heads]
    out_shape = [jax.ShapeDtypeStruct((nb, t * nh, V_DIM), F32)] * 2
    if prompt:
        nc = rows // kt
        hm = pl.BlockSpec((1, nh, rows, V_DIM), lambda b, i: (b, 0, i, 0))
        tr = lambda r: pl.BlockSpec((1, nh, nc, r, kt), lambda b, i: (b, 0, i, 0, 0))
        out_specs += [hm, tr(V_DIM + BF16_ROWS), tr(V_DIM), hm]
        out_shape += [jax.ShapeDtypeStruct((nb, nh, t, V_DIM), BF16),
                      jax.ShapeDtypeStruct((nb, nh, t // kt, V_DIM + BF16_ROWS, kt), BF16),
                      jax.ShapeDtypeStruct((nb, nh, t // kt, V_DIM, kt), BF16),
                      jax.ShapeDtypeStruct((nb, nh, t, V_DIM), BF16)]
    else:
        out_specs += [tok, tok]
        out_shape += [jax.ShapeDtypeStruct((nb, t, w), F32)] * 2
    return pl.pallas_call(
        functools.partial(_proj_kernel, prompt=prompt, kt=kt),
        grid=(nb, t // rows),
        in_specs=[
            pl.BlockSpec((1, rows, d), lambda b, i: (b, i, 0)),
            full((1, d)), full((1, d)), full((d, 2 * w)), full((d, 2 * w)),
            full((1, w)), full((1, w)),
            pl.BlockSpec((rows, LANE), lambda b, i: (i, 0)),
            pl.BlockSpec((rows, LANE), lambda b, i: (i, 0)),
            full((w, LANE)), full((2 * LANE, w)),
        ],
        out_specs=out_specs,
        out_shape=out_shape,
        compiler_params=pltpu.CompilerParams(
            dimension_semantics=("parallel", "parallel"), vmem_limit_bytes=VMEM_LIMIT_PROJ_BYTES),
        name="proj_prompt" if prompt else "proj_sample",
    )(x, kv_norm.reshape(1, d), b_norm.reshape(1, d), kv_w.astype(BF16), b_w_half,
      gk, gq, cos_t, sin_t, gather, expand)


def _diff_lambda(lq1, lk1, lq2, lk2, lam_init):
    return (jnp.exp(jnp.sum(lq1 * lk1, axis=-1, keepdims=True))
            - jnp.exp(jnp.sum(lq2 * lk2, axis=-1, keepdims=True)) + lam_init)


def _prompt_attn_kernel(qt_ref, khm_ref, vt_ref, sg_ref, x_ref, w_out_ref, hg_ref,
                        lq1_ref, lk1_ref, lq2_ref, lk2_ref, gk_ref, o_ref,
                        qm_s, shift_s, acc_s, p_s, oh_s, *, lam_init):
    nh, vd, tq = qt_ref.shape[1], qt_ref.shape[3], qt_ref.shape[4]
    qi = pl.program_id(1)
    chains = [(h, c) for h in range(nh) for c in range(2)]

    key_norm_bound = (math.sqrt(HEAD_DIM) * KEY_NORM_SLACK
                      * jnp.max(jnp.abs(gk_ref[...]), axis=-1, keepdims=True))
    feat = lax.broadcasted_iota(jnp.int32, (vd, tq), 0)
    for h in range(nh):
        qt = qt_ref[0, h, 0]
        zero = jnp.zeros_like(qt)
        for c in range(2):
            qm = jnp.where((feat >= HEAD_DIM) if c else (feat < HEAD_DIM), qt, zero)
            qm_s[h, c] = qm
            qf = qm.astype(F32)
            shift_s[h, c] = jnp.sqrt(jnp.sum(qf * qf, axis=0, keepdims=True)) * key_norm_bound

    def visible_mask():
        krow = lax.broadcasted_iota(jnp.int32, (tq, tq), 0)
        qcol = lax.broadcasted_iota(jnp.int32, (tq, tq), 1)
        return (krow // CHUNK) <= (qcol // CHUNK)

    def key_tile(h, kt):
        return khm_ref[0, h, pl.ds(pl.multiple_of(kt * tq, tq), tq), :]

    def epilogue():
        lam = _diff_lambda(lq1_ref[...], lk1_ref[...], lq2_ref[...], lk2_ref[...], lam_init)
        for h in range(nh):
            w1 = 1.0 / acc_s[h, 0, vd:vd + 1, :]
            w2 = lam / acc_s[h, 1, vd:vd + 1, :]
            ot = acc_s[h, 0, 0:vd, :] * w1 - acc_s[h, 1, 0:vd, :] * w2
            ms = jnp.mean(ot * ot, axis=0, keepdims=True)
            on = (ot * lax.rsqrt(ms + EPS)) * hg_ref[...]
            oh_s[:, h * vd:(h + 1) * vd] = (on.T * sg_ref[0, h].astype(F32)).astype(BF16)
        o_ref[0] = x_ref[0] + jnp.dot(oh_s[...], w_out_ref[...], preferred_element_type=F32)

    def tile_update(kt, masked, assign=False):
        if masked:
            visible = visible_mask()
        for h, c in chains:
            s = jnp.dot(key_tile(h, kt), qm_s[h, c], preferred_element_type=F32)
            p = jnp.exp2(s - shift_s[h, c])
            if masked:
                p = jnp.where(visible, p, 0.0)
            p_s[h, c] = p.astype(BF16)
        for h, c in chains:
            pv = jnp.dot(vt_ref[0, h, kt], p_s[h, c], preferred_element_type=F32)
            if assign:
                acc_s[h, c] = pv
            else:
                acc_s[h, c] += pv

    def pair_body(i, carry):
        tile_update(2 * i, False)
        tile_update(2 * i + 1, False)
        return carry

    tile_update(qi, True, assign=True)
    lax.fori_loop(0, qi // 2, pair_body, 0)

    @pl.when(qi % 2 == 1)
    def _():
        tile_update(qi - 1, False)

    epilogue()

    denom_min = jnp.min(acc_s[:, :, vd:vd + 1, :])

    @pl.when(jnp.logical_not(denom_min >= DENOM_FLOOR))
    def _():
        visible = visible_mask()

        def head_body(h, carry):
            def tile_step(kt, state, masked):
                new = []
                for c, (m, acc) in enumerate(state):
                    s = jnp.dot(key_tile(h, kt), qm_s[h, c], preferred_element_type=F32)
                    if masked:
                        s = jnp.where(visible, s, NEG_BIG)
                    m_new = jnp.maximum(m, jnp.max(s, axis=0, keepdims=True))
                    p = jnp.exp2(s - m_new).astype(BF16)
                    acc_new = jnp.exp2(m - m_new) * acc + jnp.dot(
                        vt_ref[0, h, kt], p, preferred_element_type=F32)
                    new.append((m_new, acc_new))
                return tuple(new)

            init = tuple((jnp.full((1, tq), NEG_BIG, F32), jnp.zeros(acc_s.shape[2:], F32))
                         for _ in range(2))
            state = lax.fori_loop(0, qi, lambda kt, st: tile_step(kt, st, False), init)
            state = tile_step(qi, state, True)
            for c in range(2):
                acc_s[h, c] = state[c][1]
            return carry

        lax.fori_loop(0, nh, head_body, 0)
        epilogue()


def _prompt_attn(qt, khm, vt, sg, x, w_out, head_g, lq1, lk1, lq2, lk2, k_norm, *, lam_init):
    nb, nh, nq, vd, tq = qt.shape
    vde = vt.shape[3]
    t = khm.shape[2]
    d = x.shape[-1]
    w = nh * vd
    hg = jnp.broadcast_to((head_g.astype(F32) * (1.0 - lam_init))[:, None], (vd, tq))
    full = lambda shape: pl.BlockSpec(shape, lambda b, i: (0,) * len(shape))
    row = lambda a: a.reshape(1, -1).astype(F32)
    return pl.pallas_call(
        functools.partial(_prompt_attn_kernel, lam_init=lam_init),
        grid=(nb, nq),
        in_specs=[
            pl.BlockSpec((1, nh, 1, vd, tq), lambda b, i: (b, 0, i, 0, 0)),
            pl.BlockSpec((1, nh, t, vd), lambda b, i: (b, 0, 0, 0)),
            pl.BlockSpec((1, nh, nq, vde, tq), lambda b, i: (b, 0, 0, 0, 0)),
            pl.BlockSpec((1, nh, tq, vd), lambda b, i: (b, 0, i, 0)),
            pl.BlockSpec((1, tq, d), lambda b, i: (b, i, 0)),
            full((w, d)), full((vd, tq)),
            full((1, HEAD_DIM)), full((1, HEAD_DIM)), full((1, HEAD_DIM)), full((1, HEAD_DIM)),
            full((1, HEAD_DIM)),
        ],
        out_specs=pl.BlockSpec((1, tq, d), lambda b, i: (b, i, 0)),
        out_shape=jax.ShapeDtypeStruct((nb, t, d), F32),
        scratch_shapes=[
            pltpu.VMEM((nh, 2, vd, tq), BF16),
            pltpu.VMEM((nh, 2, 1, tq), F32),
            pltpu.VMEM((nh, 2, vde, tq), F32),
            pltpu.VMEM((nh, 2, tq, tq), BF16),
            pltpu.VMEM((tq, w), BF16),
        ],
        compiler_params=pltpu.CompilerParams(
            dimension_semantics=("parallel", "arbitrary"), vmem_limit_bytes=VMEM_LIMIT_BYTES),
        name="attn_prompt",
    )(qt, khm, vt, sg, x, w_out.astype(BF16), hg, row(lq1), row(lk1), row(lq2), row(lk2),
      row(k_norm))


def _sample_attn_kernel(q_ref, kn_ref, vn_ref, ck_ref, cv_ref, sg_ref, x_ref, w_out_ref, hg_ref,
                        lq1_ref, lk1_ref, lq2_ref, lk2_ref, o_ref, so_s, sn_s, po_s, pn_s, l_s,
                        *, lam_init):
    q = q_ref[0]
    tq, w = q.shape
    nh = w // V_DIM
    past = ck_ref.shape[1] // nh
    lam = _diff_lambda(lq1_ref[...], lk1_ref[...], lq2_ref[...], lk2_ref[...], lam_init)
    lane = lax.broadcasted_iota(jnp.int32, (tq, V_DIM), 1)
    contract_last = (((1,), (1,)), ((), ()))
    for h in range(nh):
        qh = q[:, h * V_DIM:(h + 1) * V_DIM]
        ck = ck_ref[0, pl.ds(h, past, stride=nh), :].astype(BF16)
        kn = kn_ref[0, pl.ds(h, tq, stride=nh), :].astype(BF16)
        q2 = jnp.concatenate([jnp.where(lane < HEAD_DIM, qh, 0.0),
                              jnp.where(lane >= HEAD_DIM, qh, 0.0)], axis=0).astype(BF16)
        so_s[h] = lax.dot_general(q2, ck, contract_last, preferred_element_type=F32)
        sn_s[h] = lax.dot_general(q2, kn, contract_last, preferred_element_type=F32)
    for h in range(nh):
        s_old, s_new = so_s[h], sn_s[h]
        m = jnp.maximum(jnp.max(s_old, axis=-1, keepdims=True),
                        jnp.max(s_new, axis=-1, keepdims=True))
        p_old = jnp.exp2(s_old - m)
        p_new = jnp.exp2(s_new - m)
        l_s[h] = jnp.sum(p_old, axis=-1, keepdims=True) + jnp.sum(p_new, axis=-1, keepdims=True)
        po_s[h] = p_old.astype(BF16)
        pn_s[h] = p_new.astype(BF16)
    outs = []
    for h in range(nh):
        cv = cv_ref[0, pl.ds(h, past, stride=nh), :].astype(BF16)
        vn = vn_ref[0, pl.ds(h, tq, stride=nh), :].astype(BF16)
        pv = (jnp.dot(po_s[h], cv, preferred_element_type=F32)
              + jnp.dot(pn_s[h], vn, preferred_element_type=F32)) / l_s[h]
        oh = pv[0:tq] - lam * pv[tq:2 * tq]
        ms = jnp.mean(oh * oh, axis=-1, keepdims=True)
        outs.append(oh * lax.rsqrt(ms + EPS) * hg_ref[...] * (1.0 - lam_init))
    o = (jnp.concatenate(outs, axis=1) * sg_ref[0]).astype(BF16)
    o_ref[0] = x_ref[0] + jnp.dot(o, w_out_ref[...], preferred_element_type=F32)


def _sample_attn(q, k_new, v_new, cache_k, cache_v, sg, x, w_out, head_g, lq1, lk1, lq2, lk2, *, lam_init):
    nb, tq, w = q.shape
    past = cache_k.shape[1]
    d = x.shape[-1]
    full = lambda shape: pl.BlockSpec(shape, lambda b: (0,) * len(shape))
    tok = lambda width: pl.BlockSpec((1, tq, width), lambda b: (b, 0, 0))
    nh = w // V_DIM
    old = pl.BlockSpec((1, past * nh, V_DIM), lambda b: (b, 0, 0))
    new = pl.BlockSpec((1, tq * nh, V_DIM), lambda b: (b, 0, 0))
    row = lambda a: a.reshape(1, -1).astype(F32)
    return pl.pallas_call(
        functools.partial(_sample_attn_kernel, lam_init=lam_init),
        grid=(nb,),
        in_specs=[tok(w), new, new, old, old, tok(w), tok(d),
                  full((w, d)), full((1, V_DIM)),
                  full((1, HEAD_DIM)), full((1, HEAD_DIM)), full((1, HEAD_DIM)), full((1, HEAD_DIM))],
        out_specs=tok(d),
        out_shape=jax.ShapeDtypeStruct((nb, tq, d), F32),
        scratch_shapes=[
            pltpu.VMEM((nh, 2 * tq, past), F32), pltpu.VMEM((nh, 2 * tq, tq), F32),
            pltpu.VMEM((nh, 2 * tq, past), BF16), pltpu.VMEM((nh, 2 * tq, tq), BF16),
            pltpu.VMEM((nh, 2 * tq, 1), F32),
        ],
        compiler_params=pltpu.CompilerParams(
            dimension_semantics=("parallel",), vmem_limit_bytes=VMEM_LIMIT_BYTES),
        name="attn_sample",
    )(q, k_new, v_new, cache_k.reshape(nb, past * nh, V_DIM), cache_v.reshape(nb, past * nh, V_DIM), sg, x,
      w_out.astype(BF16), row(head_g), row(lq1), row(lk1), row(lq2), row(lk2))


def _pick_tile(n, target):
    tile = min(n, target)
    while n % tile:
        tile //= 2
    return tile


def kernel(x_prompt, x_sample, state_conv, state_h, cache_k, cache_v, a_norm, a_w_in, a_conv_w, a_conv_b, a_gate_r_w, a_gate_r_b, a_gate_i_w, a_gate_i_b, a_lambda, a_w_out, kv_norm, kv_w, k_norm, b_norm, b_w_in, b_q_norm, b_lambda_q1, b_lambda_k1, b_lambda_q2, b_lambda_k2, b_head_norm, b_w_out):
    n_a, n_b = a_norm.shape[0], b_norm.shape[0]
    assert n_a == 1 and n_b == 1, "one recurrent layer followed by one attention layer"
    nb, t, d = x_prompt.shape
    nbs, ts, _ = x_sample.shape
    d_rnn = a_lambda.shape[-1]
    w = kv_w.shape[1] // 2
    nh = w // V_DIM
    past = cache_k.shape[1]
    halo = CONV_W - 1
    lam_init = 0.8 - 0.6 * math.exp(-0.3 * n_a)
    a_args = (a_norm[0], a_w_in[0], a_conv_w[0], a_conv_b[0], a_gate_r_w[0], a_gate_r_b[0],
              a_gate_i_w[0], a_gate_i_b[0], a_lambda[0], a_w_out[0])
    lam_args = (b_lambda_q1[0], b_lambda_k1[0], b_lambda_q2[0], b_lambda_k2[0])

    def conv_state_out(c, n):
        return jnp.swapaxes(c.reshape(halo, n, d_rnn), 0, 1)[None]

    x1, conv_p, h_p = _rnn_layer(
        x_prompt, jnp.zeros((halo * nb, d_rnn), F32), jnp.zeros((nb, d_rnn), F32), *a_args,
        reset=True, tt=_pick_tile(t, 64))
    tq = _pick_tile(t, 256)
    k_p, v_p, khm, vt, qt, sg = _proj(
        x1, jnp.arange(t, dtype=jnp.int32), kv_norm, b_norm[0], kv_w, b_w_in[0], k_norm, b_q_norm[0],
        prompt=True, rows=_pick_tile(t, 1024), kt=tq)
    y_p = _prompt_attn(qt, khm, vt, sg, x1, b_w_out[0], b_head_norm[0], *lam_args, k_norm,
                       lam_init=lam_init)

    conv0 = jnp.swapaxes(state_conv[0], 0, 1).reshape(halo * nbs, d_rnn)
    x1s, conv_s, h_s = _rnn_layer(x_sample, conv0, state_h[0], *a_args, reset=False, tt=ts)
    k_s, v_s, q_s, sg_s = _proj(
        x1s.reshape(1, nbs * ts, d), jnp.tile(past + jnp.arange(ts, dtype=jnp.int32), nbs),
        kv_norm, b_norm[0], kv_w, b_w_in[0], k_norm, b_q_norm[0], prompt=False, rows=nbs * ts, kt=ts)
    k_s, v_s = k_s.reshape(nbs, ts * nh, V_DIM), v_s.reshape(nbs, ts * nh, V_DIM)
    q_s, sg_s = q_s.reshape(nbs, ts, w), sg_s.reshape(nbs, ts, w)
    y_s = _sample_attn(q_s, k_s, v_s, cache_k, cache_v, sg_s, x1s, b_w_out[0], b_head_norm[0],
                       *lam_args, lam_init=lam_init)

    return (y_p, y_s,
            conv_state_out(conv_p, nb), h_p[None],
            k_p.reshape(nb, t, nh, V_DIM), v_p.reshape(nb, t, nh, V_DIM),
            conv_state_out(conv_s, nbs), h_s[None],
            k_s.reshape(nbs, ts, nh, V_DIM), v_s.reshape(nbs, ts, nh, V_DIM))
```

```python
import functools
import math

import numpy as np
import jax
import jax.numpy as jnp
from jax import lax
from jax.experimental import pallas as pl
from jax.experimental.pallas import tpu as pltpu

LANE = 128
SUBLANE = 8
VMEM_LIMIT_BYTES = 56 * 1024 * 1024
VMEM_LIMIT_PROJ_BYTES = 58 * 1024 * 1024

CHUNK = 64
CONV_W = 4
LRU_C = 8.0
HEAD_DIM = 64
V_DIM = 2 * HEAD_DIM
ROPE_THETA = 10000.0
EPS = 1e-6
LOG2E = 1.4426950408889634
NEG_BIG = -1e30
BF16_ROWS = 16
KEY_NORM_SLACK = 1.0 + 2.0 ** -6
DENOM_FLOOR = 2.0 ** -100

F32 = jnp.float32
BF16 = jnp.bfloat16


def _gate_band_layout(d_rnn, blk):
    windows = []
    for j in range(d_rnn // LANE):
        lo = ((LANE * j) // blk) * blk
        hi = ((LANE * j + LANE - 1) // blk + 1) * blk
        ws = (lo // LANE) * LANE
        windows.append((ws, -(-(hi - ws) // LANE) * LANE))
    assert all(ws + width <= d_rnn for ws, width in windows)
    return tuple(windows)


def _block_diag_dense(w):
    n, blk, _ = w.shape
    return jnp.concatenate(
        [jnp.pad(w[i], ((0, 0), (blk * i, blk * (n - 1 - i)))) for i in range(n)], axis=0)


def _rnn_layer_kernel(x_ref, conv0_ref, h0_ref, g_ref, w_in_ref, cw_ref, cb_ref, wg_ref, bg_ref,
                      lam_ref, w_out_ref, o_ref, conv_o_ref, h_o_ref,
                      ubuf, a_s, b_s, y_s, hc, *, reset, windows):
    nb, tt, d = x_ref.shape
    rows = nb * tt
    d_rnn = lam_ref.shape[1]
    halo = (CONV_W - 1) * nb
    step = pl.program_id(0)

    @pl.when(step == 0)
    def _():
        ubuf[0:halo, :] = conv0_ref[...]
        hc[...] = h0_ref[...]

    x = jnp.swapaxes(x_ref[...], 0, 1).reshape(rows, d)
    ms = jnp.mean(x * x, axis=-1, keepdims=True)
    xn = (x * lax.rsqrt(ms + EPS) * g_ref[...]).astype(BF16)
    h = jnp.dot(xn, w_in_ref[...], preferred_element_type=F32)
    ubuf[halo:halo + rows, :] = h[:, :d_rnn]
    half_gate = h[:, d_rnn:]

    u = cb_ref[...] + cw_ref[CONV_W - 1:CONV_W, :] * ubuf[halo:halo + rows, :]
    for j in range(CONV_W - 1):
        u = u + cw_ref[j:j + 1, :] * ubuf[j * nb:j * nb + rows, :]
    ubuf[0:halo, :] = ubuf[rows:rows + halo, :]
    conv_o_ref[...] = ubuf[0:halo, :]

    lam = lam_ref[...]
    softplus_neg = jnp.maximum(-lam, 0.0) + jnp.log1p(jnp.exp(-jnp.abs(lam)))
    c2 = (-0.5 * LOG2E * LRU_C) * softplus_neg

    u_bf = u.astype(BF16)
    for j, (ws, width) in enumerate(windows):
        cs = slice(j * LANE, (j + 1) * LANE)
        ri = jnp.dot(u_bf[:, ws:ws + width], wg_ref[j, 0:width, :],
                     preferred_element_type=F32) + bg_ref[j]
        t_r = jnp.tanh(ri[:, :LANE])
        t_i = jnp.tanh(ri[:, LANE:])
        a = jnp.exp2(c2[:, cs] * t_r + c2[:, cs])
        y = 1.0 - a * a
        mult = jnp.where(y > 0.0, y * lax.rsqrt(y), 0.0)
        if reset:
            head = jnp.where(step == 0, 1.0, mult[0:nb])
            mult = jnp.concatenate([head, mult[nb:]], axis=0)
        a_s[:, cs] = a
        b_s[:, cs] = (mult * u[:, cs]) * (0.5 * t_i + 0.5)

    def scan_step(t, hcur):
        r0 = pl.multiple_of(t * nb, SUBLANE)
        hn = a_s[pl.ds(r0, nb), :] * hcur + b_s[pl.ds(r0, nb), :]
        y_s[pl.ds(r0, nb), :] = hn
        return hn

    hlast = lax.fori_loop(0, tt, scan_step, hc[...], unroll=min(tt, 8))
    hc[...] = hlast
    h_o_ref[...] = hlast

    yg = (y_s[...] * (half_gate * (jnp.tanh(half_gate) + 1.0))).astype(BF16)
    out = x + jnp.dot(yg, w_out_ref[...], preferred_element_type=F32)
    o_ref[...] = jnp.swapaxes(out.reshape(tt, nb, d), 0, 1)


def _rnn_layer(x, conv0, h0, norm_g, w_in, conv_w, conv_b, wr, br, wi, bi, lam, w_out, *, reset, tt):
    nb, t, d = x.shape
    assert nb == SUBLANE and t % tt == 0 and tt % SUBLANE == 0
    d_rnn = lam.shape[-1]
    n_blk, blk, _ = wr.shape
    windows = _gate_band_layout(d_rnn, blk)
    gate_k = max(width for _, width in windows)
    ncol = d_rnn // LANE
    rows = nb * tt
    halo = (CONV_W - 1) * nb

    dense_r = _block_diag_dense(wr)
    dense_i = _block_diag_dense(wi)
    wg = jnp.stack([
        jnp.pad(jnp.concatenate([dense_r[ws:ws + width, j * LANE:(j + 1) * LANE],
                                 dense_i[ws:ws + width, j * LANE:(j + 1) * LANE]], axis=1),
                ((0, gate_k - width), (0, 0)))
        for j, (ws, width) in enumerate(windows)])
    bg = jnp.concatenate([br.reshape(ncol, 1, LANE), bi.reshape(ncol, 1, LANE)], axis=2)
    wg = (0.5 * wg).astype(BF16)
    bg = 0.5 * bg
    gate_cols = jnp.arange(2 * d_rnn) >= d_rnn
    w_in = (w_in * jnp.where(gate_cols, 0.5, 1.0)).astype(BF16)

    full = lambda shape: pl.BlockSpec(shape, lambda i: (0,) * len(shape))
    kern = functools.partial(_rnn_layer_kernel, reset=reset, windows=windows)
    return pl.pallas_call(
        kern,
        grid=(t // tt,),
        in_specs=[
            pl.BlockSpec((nb, tt, d), lambda i: (0, i, 0)),
            full((halo, d_rnn)), full((nb, d_rnn)), full((1, d)),
            full((d, 2 * d_rnn)), full((CONV_W, d_rnn)), full((1, d_rnn)),
            full((ncol, gate_k, 2 * LANE)), full((ncol, 1, 2 * LANE)), full((1, d_rnn)),
            full((d_rnn, d)),
        ],
        out_specs=[
            pl.BlockSpec((nb, tt, d), lambda i: (0, i, 0)),
            full((halo, d_rnn)), full((nb, d_rnn)),
        ],
        out_shape=[
            jax.ShapeDtypeStruct((nb, t, d), F32),
            jax.ShapeDtypeStruct((halo, d_rnn), F32),
            jax.ShapeDtypeStruct((nb, d_rnn), F32),
        ],
        scratch_shapes=[
            pltpu.VMEM((halo + rows, d_rnn), F32),
            pltpu.VMEM((rows, d_rnn), F32),
            pltpu.VMEM((rows, d_rnn), F32),
            pltpu.VMEM((rows, d_rnn), F32),
            pltpu.VMEM((nb, d_rnn), F32),
        ],
        compiler_params=pltpu.CompilerParams(
            dimension_semantics=("arbitrary",), vmem_limit_bytes=VMEM_LIMIT_BYTES),
        name="rnn_layer_reset" if reset else "rnn_layer_state",
    )(x, conv0, h0, norm_g.reshape(1, d), w_in, conv_w, conv_b.reshape(1, d_rnn),
      wg, bg, lam.reshape(1, d_rnn), w_out.astype(BF16))


def _group_sum_matrices(width):
    groups = width // HEAD_DIM
    assert groups <= LANE
    gather = np.zeros((width, LANE), np.float32)
    gather[np.arange(width), np.arange(width) // HEAD_DIM] = 1.0
    expand = np.concatenate([gather.T, gather.T], axis=0)
    return jnp.asarray(gather, BF16), jnp.asarray(expand, BF16)


def _group_rms_scale(x, gather, expand):
    ssum = jnp.dot((x * x).astype(BF16), gather, preferred_element_type=F32)
    inv = lax.rsqrt(ssum * (1.0 / HEAD_DIM) + EPS)
    hi = inv.astype(BF16)
    lo = (inv - hi.astype(F32)).astype(BF16)
    return jnp.dot(jnp.concatenate([hi, lo], axis=1), expand, preferred_element_type=F32)


def _rope_slab(xh, cos_t, sin_t):
    lane = lax.broadcasted_iota(jnp.int32, xh.shape, 1)
    first_half = (lane % HEAD_DIM) < (HEAD_DIM // 2)
    partner = jnp.where(first_half, pltpu.roll(xh, LANE - HEAD_DIM // 2, 1),
                        pltpu.roll(xh, HEAD_DIM // 2, 1))
    return xh * cos_t + partner * sin_t


def _proj_kernel(x_ref, gkv_ref, gb_ref, kvw_ref, bw_ref, gk_ref, gq_ref, cos_ref, sin_ref,
                 gather_ref, expand_ref, *out_refs, prompt, kt):
    if prompt:
        k_ref, v_ref, khm_ref, vt_ref, qt_ref, sg_ref = out_refs
    else:
        k_ref, v_ref, q_ref, sg_ref = out_refs
    rows = x_ref.shape[1]
    w = gk_ref.shape[1]
    nh = w // V_DIM
    sub = kt if prompt else rows
    gather, expand = gather_ref[...], expand_ref[...]
    if prompt:
        pad_row = lax.broadcasted_iota(jnp.int32, (BF16_ROWS, kt), 0)
        ones_row = jnp.where(pad_row == 0, 1.0, 0.0).astype(BF16)
    for c in range(rows // sub):
        rs = slice(c * sub, (c + 1) * sub)
        x = x_ref[0, rs, :]
        ms = jnp.mean(x * x, axis=-1, keepdims=True)
        xh = x * lax.rsqrt(ms + EPS)
        kv = jnp.dot((xh * gkv_ref[...]).astype(BF16), kvw_ref[...], preferred_element_type=F32)
        qg = jnp.dot((xh * gb_ref[...]).astype(BF16), bw_ref[...], preferred_element_type=F32)
        k_raw, v = kv[:, :w], kv[:, w:]
        q_raw, half_gate = qg[:, :w], qg[:, w:]
        kn = k_raw * _group_rms_scale(k_raw, gather, expand) * gk_ref[...]
        qn = q_raw * _group_rms_scale(q_raw, gather, expand) * gq_ref[...]
        sg = half_gate * (jnp.tanh(half_gate) + 1.0)
        cos_t, sin_t = cos_ref[rs, :], sin_ref[rs, :]
        if not prompt:
            sg_ref[0] = sg
        for h in range(nh):
            cs = slice(h * V_DIM, (h + 1) * V_DIM)
            kh = _rope_slab(kn[:, cs], cos_t, sin_t)
            qh = _rope_slab(qn[:, cs], cos_t, sin_t)
            k_ref[0, pl.ds(c * sub * nh + h, sub, stride=nh), :] = kh
            v_ref[0, pl.ds(c * sub * nh + h, sub, stride=nh), :] = v[:, cs]
            if prompt:
                khm_ref[0, h, rs, :] = kh.astype(BF16)
                sg_ref[0, h, rs, :] = sg[:, cs].astype(BF16)
                vt_ref[0, h, c, 0:V_DIM, :] = v[:, cs].T.astype(BF16)
                vt_ref[0, h, c, V_DIM:V_DIM + BF16_ROWS, :] = ones_row
                qt_ref[0, h, c] = qh.T.astype(BF16)
            else:
                q_ref[0, :, cs] = qh


def _rope_tables(pos):
    half = HEAD_DIM // 2
    inv = ROPE_THETA ** (-jnp.arange(half, dtype=F32) / half)
    ang = pos.astype(F32)[:, None] * inv[None, :]
    cos, sin = jnp.cos(ang), jnp.sin(ang)
    reps = LANE // HEAD_DIM
    return (jnp.tile(jnp.concatenate([cos, cos], axis=1), (1, reps)),
            jnp.tile(jnp.concatenate([-sin, sin], axis=1), (1, reps)))


def _proj(x, pos, kv_norm, b_norm, kv_w, b_w_in, k_norm, q_norm, *, prompt, rows, kt):
    nb, t, d = x.shape
    w = kv_w.shape[1] // 2
    nh = w // V_DIM
    assert t % rows == 0 and (not prompt or rows % kt == 0)
    cos_t, sin_t = _rope_tables(pos)
    gather, expand = _group_sum_matrices(w)
    reps = w // HEAD_DIM
    gk = jnp.tile(k_norm.astype(F32), reps).reshape(1, w)
    gq = (jnp.tile(q_norm.astype(F32), reps) * (HEAD_DIM ** -0.5 * LOG2E)).reshape(1, w)
    b_w_half = (b_w_in * jnp.where(jnp.arange(2 * w) >= w, 0.5, 1.0)).astype(BF16)

    full = lambda shape: pl.BlockSpec(shape, lambda b, i: (0,) * len(shape),
                                      pipeline_mode=pl.Buffered(1))
    tok = pl.BlockSpec((1, rows, w), lambda b, i: (b, i, 0))
    tok_heads = pl.BlockSpec((1, rows * nh, V_DIM), lambda b, i: (b, i, 0))
    out_specs = [tok_heads, tok_heads]
    out_shape = [jax.ShapeDtypeStruct((nb, t * nh, V_DIM), F32)] * 2
    if prompt:
        nc = rows // kt
        hm = pl.BlockSpec((1, nh, rows, V_DIM), lambda b, i: (b, 0, i, 0))
        tr = lambda r: pl.BlockSpec((1, nh, nc, r, kt), lambda b, i: (b, 0, i, 0, 0))
        out_specs += [hm, tr(V_DIM + BF16_ROWS), tr(V_DIM), hm]
        out_shape += [jax.ShapeDtypeStruct((nb, nh, t, V_DIM), BF16),
                      jax.ShapeDtypeStruct((nb, nh, t // kt, V_DIM + BF16_ROWS, kt), BF16),
                      jax.ShapeDtypeStruct((nb, nh, t // kt, V_DIM, kt), BF16),
                      jax.ShapeDtypeStruct((nb, nh, t, V_DIM), BF16)]
    else:
        out_specs += [tok, tok]
        out_shape += [jax.ShapeDtypeStruct((nb, t, w), F32)] * 2
    return pl.pallas_call(
        functools.partial(_proj_kernel, prompt=prompt, kt=kt),
        grid=(nb, t // rows),
        in_specs=[
            pl.BlockSpec((1, rows, d), lambda b, i: (b, i, 0)),
            full((1, d)), full((1, d)), full((d, 2 * w)), full((d, 2 * w)),
            full((1, w)), full((1, w)),
            pl.BlockSpec((rows, LANE), lambda b, i: (i, 0)),
            pl.BlockSpec((rows, LANE), lambda b, i: (i, 0)),
            full((w, LANE)), full((2 * LANE, w)),
        ],
        out_specs=out_specs,
        out_shape=out_shape,
        compiler_params=pltpu.CompilerParams(
            dimension_semantics=("parallel", "parallel"), vmem_limit_bytes=VMEM_LIMIT_PROJ_BYTES),
        name="proj_prompt" if prompt else "proj_sample",
    )(x, kv_norm.reshape(1, d), b_norm.reshape(1, d), kv_w.astype(BF16), b_w_half,
      gk, gq, cos_t, sin_t, gather, expand)


def _diff_lambda(lq1, lk1, lq2, lk2, lam_init):
    return (jnp.exp(jnp.sum(lq1 * lk1, axis=-1, keepdims=True))
            - jnp.exp(jnp.sum(lq2 * lk2, axis=-1, keepdims=True)) + lam_init)


def _prompt_attn_kernel(qt_ref, khm_ref, vt_ref, sg_ref, x_ref, w_out_ref, hg_ref,
                        lq1_ref, lk1_ref, lq2_ref, lk2_ref, gk_ref, o_ref,
                        qm_s, shift_s, acc_s, p_s, oh_s, *, lam_init):
    nh, vd, tq = qt_ref.shape[1], qt_ref.shape[3], qt_ref.shape[4]
    qi = pl.program_id(1)
    chains = [(h, c) for h in range(nh) for c in range(2)]

    key_norm_bound = (math.sqrt(HEAD_DIM) * KEY_NORM_SLACK
                      * jnp.max(jnp.abs(gk_ref[...]), axis=-1, keepdims=True))
    feat = lax.broadcasted_iota(jnp.int32, (vd, tq), 0)
    for h in range(nh):
        qt = qt_ref[0, h, 0]
        zero = jnp.zeros_like(qt)
        for c in range(2):
            qm = jnp.where((feat >= HEAD_DIM) if c else (feat < HEAD_DIM), qt, zero)
            qm_s[h, c] = qm
            qf = qm.astype(F32)
            shift_s[h, c] = jnp.sqrt(jnp.sum(qf * qf, axis=0, keepdims=True)) * key_norm_bound

    def visible_mask():
        krow = lax.broadcasted_iota(jnp.int32, (tq, tq), 0)
        qcol = lax.broadcasted_iota(jnp.int32, (tq, tq), 1)
        return (krow // CHUNK) <= (qcol // CHUNK)

    def key_tile(h, kt):
        return khm_ref[0, h, pl.ds(pl.multiple_of(kt * tq, tq), tq), :]

    def epilogue():
        lam = _diff_lambda(lq1_ref[...], lk1_ref[...], lq2_ref[...], lk2_ref[...], lam_init)
        for h in range(nh):
            w1 = 1.0 / acc_s[h, 0, vd:vd + 1, :]
            w2 = lam / acc_s[h, 1, vd:vd + 1, :]
            ot = acc_s[h, 0, 0:vd, :] * w1 - acc_s[h, 1, 0:vd, :] * w2
            ms = jnp.mean(ot * ot, axis=0, keepdims=True)
            on = (ot * lax.rsqrt(ms + EPS)) * hg_ref[...]
            oh_s[:, h * vd:(h + 1) * vd] = (on.T * sg_ref[0, h].astype(F32)).astype(BF16)
        o_ref[0] = x_ref[0] + jnp.dot(oh_s[...], w_out_ref[...], preferred_element_type=F32)

    def tile_update(kt, masked):
        if masked:
            visible = visible_mask()
        for h, c in chains:
            s = jnp.dot(key_tile(h, kt), qm_s[h, c], preferred_element_type=F32)
            p = jnp.exp2(s - shift_s[h, c])
            if masked:
                p = jnp.where(visible, p, 0.0)
            p_s[h, c] = p.astype(BF16)
        for h, c in chains:
            acc_s[h, c] += jnp.dot(vt_ref[0, h, kt], p_s[h, c], preferred_element_type=F32)

    def group_body(n, i, carry):
        for j in range(n):
            tile_update(n * i + j, False)
        return carry

    acc_s[...] = jnp.zeros(acc_s.shape, F32)
    lax.fori_loop(0, qi // 4, functools.partial(group_body, 4), 0)

    @pl.when(qi % 4 >= 2)
    def _():
        group_body(2, 2 * (qi // 4), 0)

    @pl.when(qi % 2 == 1)
    def _():
        tile_update(qi - 1, False)

    tile_update(qi, True)
    epilogue()

    denom_min = jnp.min(acc_s[:, :, vd:vd + 1, :])

    @pl.when(jnp.logical_not(denom_min >= DENOM_FLOOR))
    def _():
        visible = visible_mask()

        def head_body(h, carry):
            def tile_step(kt, state, masked):
                new = []
                for c, (m, acc) in enumerate(state):
                    s = jnp.dot(key_tile(h, kt), qm_s[h, c], preferred_element_type=F32)
                    if masked:
                        s = jnp.where(visible, s, NEG_BIG)
                    m_new = jnp.maximum(m, jnp.max(s, axis=0, keepdims=True))
                    p = jnp.exp2(s - m_new).astype(BF16)
                    acc_new = jnp.exp2(m - m_new) * acc + jnp.dot(
                        vt_ref[0, h, kt], p, preferred_element_type=F32)
                    new.append((m_new, acc_new))
                return tuple(new)

            init = tuple((jnp.full((1, tq), NEG_BIG, F32), jnp.zeros(acc_s.shape[2:], F32))
                         for _ in range(2))
            state = lax.fori_loop(0, qi, lambda kt, st: tile_step(kt, st, False), init)
            state = tile_step(qi, state, True)
            for c in range(2):
                acc_s[h, c] = state[c][1]
            return carry

        lax.fori_loop(0, nh, head_body, 0)
        epilogue()


def _prompt_attn(qt, khm, vt, sg, x, w_out, head_g, lq1, lk1, lq2, lk2, k_norm, *, lam_init):
    nb, nh, nq, vd, tq = qt.shape
    vde = vt.shape[3]
    t = khm.shape[2]
    d = x.shape[-1]
    w = nh * vd
    hg = jnp.broadcast_to((head_g.astype(F32) * (1.0 - lam_init))[:, None], (vd, tq))
    full = lambda shape: pl.BlockSpec(shape, lambda b, i: (0,) * len(shape))
    row = lambda a: a.reshape(1, -1).astype(F32)
    return pl.pallas_call(
        functools.partial(_prompt_attn_kernel, lam_init=lam_init),
        grid=(nb, nq),
        in_specs=[
            pl.BlockSpec((1, nh, 1, vd, tq), lambda b, i: (b, 0, i, 0, 0)),
            pl.BlockSpec((1, nh, t, vd), lambda b, i: (b, 0, 0, 0)),
            pl.BlockSpec((1, nh, nq, vde, tq), lambda b, i: (b, 0, 0, 0, 0)),
            pl.BlockSpec((1, nh, tq, vd), lambda b, i: (b, 0, i, 0)),
            pl.BlockSpec((1, tq, d), lambda b, i: (b, i, 0)),
            full((w, d)), full((vd, tq)),
            full((1, HEAD_DIM)), full((1, HEAD_DIM)), full((1, HEAD_DIM)), full((1, HEAD_DIM)),
            full((1, HEAD_DIM)),
        ],
        out_specs=pl.BlockSpec((1, tq, d), lambda b, i: (b, i, 0)),
        out_shape=jax.ShapeDtypeStruct((nb, t, d), F32),
        scratch_shapes=[
            pltpu.VMEM((nh, 2, vd, tq), BF16),
            pltpu.VMEM((nh, 2, 1, tq), F32),
            pltpu.VMEM((nh, 2, vde, tq), F32),
            pltpu.VMEM((nh, 2, tq, tq), BF16),
            pltpu.VMEM((tq, w), BF16),
        ],
        compiler_params=pltpu.CompilerParams(
            dimension_semantics=("parallel", "arbitrary"), vmem_limit_bytes=VMEM_LIMIT_BYTES),
        name="attn_prompt",
    )(qt, khm, vt, sg, x, w_out.astype(BF16), hg, row(lq1), row(lk1), row(lq2), row(lk2),
      row(k_norm))


def _sample_attn_kernel(q_ref, kn_ref, vn_ref, ck_ref, cv_ref, sg_ref, x_ref, w_out_ref, hg_ref,
                        lq1_ref, lk1_ref, lq2_ref, lk2_ref, o_ref, so_s, sn_s, po_s, pn_s, l_s,
                        *, lam_init):
    q = q_ref[0]
    tq, w = q.shape
    nh = w // V_DIM
    past = ck_ref.shape[1] // nh
    lam = _diff_lambda(lq1_ref[...], lk1_ref[...], lq2_ref[...], lk2_ref[...], lam_init)
    lane = lax.broadcasted_iota(jnp.int32, (tq, V_DIM), 1)
    contract_last = (((1,), (1,)), ((), ()))
    for h in range(nh):
        qh = q[:, h * V_DIM:(h + 1) * V_DIM]
        ck = ck_ref[0, pl.ds(h, past, stride=nh), :].astype(BF16)
        kn = kn_ref[0, pl.ds(h, tq, stride=nh), :].astype(BF16)
        q2 = jnp.concatenate([jnp.where(lane < HEAD_DIM, qh, 0.0),
                              jnp.where(lane >= HEAD_DIM, qh, 0.0)], axis=0).astype(BF16)
        so_s[h] = lax.dot_general(q2, ck, contract_last, preferred_element_type=F32)
        sn_s[h] = lax.dot_general(q2, kn, contract_last, preferred_element_type=F32)
    for h in range(nh):
        s_old, s_new = so_s[h], sn_s[h]
        m = jnp.maximum(jnp.max(s_old, axis=-1, keepdims=True),
                        jnp.max(s_new, axis=-1, keepdims=True))
        p_old = jnp.exp2(s_old - m)
        p_new = jnp.exp2(s_new - m)
        l_s[h] = jnp.sum(p_old, axis=-1, keepdims=True) + jnp.sum(p_new, axis=-1, keepdims=True)
        po_s[h] = p_old.astype(BF16)
        pn_s[h] = p_new.astype(BF16)
    outs = []
    for h in range(nh):
        cv = cv_ref[0, pl.ds(h, past, stride=nh), :].astype(BF16)
        vn = vn_ref[0, pl.ds(h, tq, stride=nh), :].astype(BF16)
        pv = (jnp.dot(po_s[h], cv, preferred_element_type=F32)
              + jnp.dot(pn_s[h], vn, preferred_element_type=F32)) / l_s[h]
        oh = pv[0:tq] - lam * pv[tq:2 * tq]
        ms = jnp.mean(oh * oh, axis=-1, keepdims=True)
        outs.append(oh * lax.rsqrt(ms + EPS) * hg_ref[...] * (1.0 - lam_init))
    o = (jnp.concatenate(outs, axis=1) * sg_ref[0]).astype(BF16)
    o_ref[0] = x_ref[0] + jnp.dot(o, w_out_ref[...], preferred_element_type=F32)


def _sample_attn(q, k_new, v_new, cache_k, cache_v, sg, x, w_out, head_g, lq1, lk1, lq2, lk2, *, lam_init):
    nb, tq, w = q.shape
    past = cache_k.shape[1]
    d = x.shape[-1]
    full = lambda shape: pl.BlockSpec(shape, lambda b: (0,) * len(shape))
    tok = lambda width: pl.BlockSpec((1, tq, width), lambda b: (b, 0, 0))
    nh = w // V_DIM
    old = pl.BlockSpec((1, past * nh, V_DIM), lambda b: (b, 0, 0))
    new = pl.BlockSpec((1, tq * nh, V_DIM), lambda b: (b, 0, 0))
    row = lambda a: a.reshape(1, -1).astype(F32)
    return pl.pallas_call(
        functools.partial(_sample_attn_kernel, lam_init=lam_init),
        grid=(nb,),
        in_specs=[tok(w), new, new, old, old, tok(w), tok(d),
                  full((w, d)), full((1, V_DIM)),
                  full((1, HEAD_DIM)), full((1, HEAD_DIM)), full((1, HEAD_DIM)), full((1, HEAD_DIM))],
        out_specs=tok(d),
        out_shape=jax.ShapeDtypeStruct((nb, tq, d), F32),
        scratch_shapes=[
            pltpu.VMEM((nh, 2 * tq, past), F32), pltpu.VMEM((nh, 2 * tq, tq), F32),
            pltpu.VMEM((nh, 2 * tq, past), BF16), pltpu.VMEM((nh, 2 * tq, tq), BF16),
            pltpu.VMEM((nh, 2 * tq, 1), F32),
        ],
        compiler_params=pltpu.CompilerParams(
            dimension_semantics=("parallel",), vmem_limit_bytes=VMEM_LIMIT_BYTES),
        name="attn_sample",
    )(q, k_new, v_new, cache_k.reshape(nb, past * nh, V_DIM), cache_v.reshape(nb, past * nh, V_DIM), sg, x,
      w_out.astype(BF16), row(head_g), row(lq1), row(lk1), row(lq2), row(lk2))


def _pick_tile(n, target):
    tile = min(n, target)
    while n % tile:
        tile //= 2
    return tile


def kernel(x_prompt, x_sample, state_conv, state_h, cache_k, cache_v, a_norm, a_w_in, a_conv_w, a_conv_b, a_gate_r_w, a_gate_r_b, a_gate_i_w, a_gate_i_b, a_lambda, a_w_out, kv_norm, kv_w, k_norm, b_norm, b_w_in, b_q_norm, b_lambda_q1, b_lambda_k1, b_lambda_q2, b_lambda_k2, b_head_norm, b_w_out):
    n_a, n_b = a_norm.shape[0], b_norm.shape[0]
    assert n_a == 1 and n_b == 1, "one recurrent layer followed by one attention layer"
    nb, t, d = x_prompt.shape
    nbs, ts, _ = x_sample.shape
    d_rnn = a_lambda.shape[-1]
    w = kv_w.shape[1] // 2
    nh = w // V_DIM
    past = cache_k.shape[1]
    halo = CONV_W - 1
    lam_init = 0.8 - 0.6 * math.exp(-0.3 * n_a)
    a_args = (a_norm[0], a_w_in[0], a_conv_w[0], a_conv_b[0], a_gate_r_w[0], a_gate_r_b[0],
              a_gate_i_w[0], a_gate_i_b[0], a_lambda[0], a_w_out[0])
    lam_args = (b_lambda_q1[0], b_lambda_k1[0], b_lambda_q2[0], b_lambda_k2[0])

    def conv_state_out(c, n):
        return jnp.swapaxes(c.reshape(halo, n, d_rnn), 0, 1)[None]

    x1, conv_p, h_p = _rnn_layer(
        x_prompt, jnp.zeros((halo * nb, d_rnn), F32), jnp.zeros((nb, d_rnn), F32), *a_args,
        reset=True, tt=_pick_tile(t, 64))
    tq = _pick_tile(t, 256)
    k_p, v_p, khm, vt, qt, sg = _proj(
        x1, jnp.arange(t, dtype=jnp.int32), kv_norm, b_norm[0], kv_w, b_w_in[0], k_norm, b_q_norm[0],
        prompt=True, rows=_pick_tile(t, 1024), kt=tq)
    y_p = _prompt_attn(qt, khm, vt, sg, x1, b_w_out[0], b_head_norm[0], *lam_args, k_norm,
                       lam_init=lam_init)

    conv0 = jnp.swapaxes(state_conv[0], 0, 1).reshape(halo * nbs, d_rnn)
    x1s, conv_s, h_s = _rnn_layer(x_sample, conv0, state_h[0], *a_args, reset=False, tt=ts)
    k_s, v_s, q_s, sg_s = _proj(
        x1s.reshape(1, nbs * ts, d), jnp.tile(past + jnp.arange(ts, dtype=jnp.int32), nbs),
        kv_norm, b_norm[0], kv_w, b_w_in[0], k_norm, b_q_norm[0], prompt=False, rows=nbs * ts, kt=ts)
    k_s, v_s = k_s.reshape(nbs, ts * nh, V_DIM), v_s.reshape(nbs, ts * nh, V_DIM)
    q_s, sg_s = q_s.reshape(nbs, ts, w), sg_s.reshape(nbs, ts, w)
    y_s = _sample_attn(q_s, k_s, v_s, cache_k, cache_v, sg_s, x1s, b_w_out[0], b_head_norm[0],
                       *lam_args, lam_init=lam_init)

    return (y_p, y_s,
            conv_state_out(conv_p, nb), h_p[None],
            k_p.reshape(nb, t, nh, V_DIM), v_p.reshape(nb, t, nh, V_DIM),
            conv_state_out(conv_s, nbs), h_s[None],
            k_s.reshape(nbs, ts, nh, V_DIM), v_s.reshape(nbs, ts, nh, V_DIM))
```

```python
import functools
import math

import numpy as np
import jax
import jax.numpy as jnp
from jax import lax
from jax.experimental import pallas as pl
from jax.experimental.pallas import tpu as pltpu

LANE = 128
SUBLANE = 8
VMEM_LIMIT_BYTES = 56 * 1024 * 1024
VMEM_LIMIT_PROJ_BYTES = 58 * 1024 * 1024

CHUNK = 64
CONV_W = 4
LRU_C = 8.0
HEAD_DIM = 64
V_DIM = 2 * HEAD_DIM
ROPE_THETA = 10000.0
EPS = 1e-6
LOG2E = 1.4426950408889634
NEG_BIG = -1e30
BF16_ROWS = 16
KEY_NORM_SLACK = 1.0 + 2.0 ** -6
DENOM_FLOOR = 2.0 ** -100

F32 = jnp.float32
BF16 = jnp.bfloat16


def _gate_band_layout(d_rnn, blk):
    windows = []
    for j in range(d_rnn // LANE):
        lo = ((LANE * j) // blk) * blk
        hi = ((LANE * j + LANE - 1) // blk + 1) * blk
        ws = (lo // LANE) * LANE
        windows.append((ws, -(-(hi - ws) // LANE) * LANE))
    assert all(ws + width <= d_rnn for ws, width in windows)
    return tuple(windows)


def _block_diag_dense(w):
    n, blk, _ = w.shape
    return jnp.concatenate(
        [jnp.pad(w[i], ((0, 0), (blk * i, blk * (n - 1 - i)))) for i in range(n)], axis=0)


def _rnn_layer_kernel(x_ref, conv0_ref, h0_ref, g_ref, w_in_ref, cw_ref, cb_ref, wg_ref, bg_ref,
                      lam_ref, w_out_ref, o_ref, conv_o_ref, h_o_ref,
                      ubuf, yg_s, hc, *, reset, windows):
    nb, tt, d = x_ref.shape
    rows = nb * tt
    d_rnn = lam_ref.shape[1]
    halo = (CONV_W - 1) * nb
    step = pl.program_id(0)

    @pl.when(step == 0)
    def _():
        ubuf[0:halo, :] = conv0_ref[...]
        hc[...] = h0_ref[...]

    x = jnp.swapaxes(x_ref[...], 0, 1).reshape(rows, d)
    ms = jnp.mean(x * x, axis=-1, keepdims=True)
    xn = (x * lax.rsqrt(ms + EPS) * g_ref[...]).astype(BF16)
    h = jnp.dot(xn, w_in_ref[...], preferred_element_type=F32)
    ubuf[halo:halo + rows, :] = h[:, :d_rnn]
    half_gate = h[:, d_rnn:]

    u = cb_ref[...] + cw_ref[CONV_W - 1:CONV_W, :] * ubuf[halo:halo + rows, :]
    for j in range(CONV_W - 1):
        u = u + cw_ref[j:j + 1, :] * ubuf[j * nb:j * nb + rows, :]
    ubuf[0:halo, :] = ubuf[rows:rows + halo, :]
    conv_o_ref[...] = ubuf[0:halo, :]

    lam = lam_ref[...]
    softplus_neg = jnp.maximum(-lam, 0.0) + jnp.log1p(jnp.exp(-jnp.abs(lam)))
    c2 = (-0.5 * LOG2E * LRU_C) * softplus_neg

    u_bf = u.astype(BF16)
    for j, (ws, width) in enumerate(windows):
        cs = slice(j * LANE, (j + 1) * LANE)
        ri = jnp.dot(u_bf[:, ws:ws + width], wg_ref[j, 0:width, :],
                     preferred_element_type=F32) + bg_ref[j]
        t_r = jnp.tanh(ri[:, :LANE])
        t_i = jnp.tanh(ri[:, LANE:])
        a = jnp.exp2(c2[:, cs] * t_r + c2[:, cs])
        y = 1.0 - a * a
        mult = jnp.where(y > 0.0, y * lax.rsqrt(y), 0.0)
        if reset:
            head = jnp.where(step == 0, 1.0, mult[0:nb])
            mult = jnp.concatenate([head, mult[nb:]], axis=0)
        b = (mult * u[:, cs]) * (0.5 * t_i + 0.5)
        hcur = hc[:, cs]
        hs = []
        for t in range(tt):
            hcur = a[t * nb:(t + 1) * nb] * hcur + b[t * nb:(t + 1) * nb]
            hs.append(hcur)
        hc[:, cs] = hcur
        hg = half_gate[:, cs]
        yg_s[:, cs] = (jnp.concatenate(hs, axis=0) * (hg * (jnp.tanh(hg) + 1.0))).astype(BF16)

    h_o_ref[...] = hc[...]
    out = x + jnp.dot(yg_s[...], w_out_ref[...], preferred_element_type=F32)
    o_ref[...] = jnp.swapaxes(out.reshape(tt, nb, d), 0, 1)


def _rnn_layer(x, conv0, h0, norm_g, w_in, conv_w, conv_b, wr, br, wi, bi, lam, w_out, *, reset, tt):
    nb, t, d = x.shape
    assert nb == SUBLANE and t % tt == 0 and tt % SUBLANE == 0
    d_rnn = lam.shape[-1]
    n_blk, blk, _ = wr.shape
    windows = _gate_band_layout(d_rnn, blk)
    gate_k = max(width for _, width in windows)
    ncol = d_rnn // LANE
    rows = nb * tt
    halo = (CONV_W - 1) * nb

    dense_r = _block_diag_dense(wr)
    dense_i = _block_diag_dense(wi)
    wg = jnp.stack([
        jnp.pad(jnp.concatenate([dense_r[ws:ws + width, j * LANE:(j + 1) * LANE],
                                 dense_i[ws:ws + width, j * LANE:(j + 1) * LANE]], axis=1),
                ((0, gate_k - width), (0, 0)))
        for j, (ws, width) in enumerate(windows)])
    bg = jnp.concatenate([br.reshape(ncol, 1, LANE), bi.reshape(ncol, 1, LANE)], axis=2)
    wg = (0.5 * wg).astype(BF16)
    bg = 0.5 * bg
    gate_cols = jnp.arange(2 * d_rnn) >= d_rnn
    w_in = (w_in * jnp.where(gate_cols, 0.5, 1.0)).astype(BF16)

    full = lambda shape: pl.BlockSpec(shape, lambda i: (0,) * len(shape))
    kern = functools.partial(_rnn_layer_kernel, reset=reset, windows=windows)
    return pl.pallas_call(
        kern,
        grid=(t // tt,),
        in_specs=[
            pl.BlockSpec((nb, tt, d), lambda i: (0, i, 0)),
            full((halo, d_rnn)), full((nb, d_rnn)), full((1, d)),
            full((d, 2 * d_rnn)), full((CONV_W, d_rnn)), full((1, d_rnn)),
            full((ncol, gate_k, 2 * LANE)), full((ncol, 1, 2 * LANE)), full((1, d_rnn)),
            full((d_rnn, d)),
        ],
        out_specs=[
            pl.BlockSpec((nb, tt, d), lambda i: (0, i, 0)),
            full((halo, d_rnn)), full((nb, d_rnn)),
        ],
        out_shape=[
            jax.ShapeDtypeStruct((nb, t, d), F32),
            jax.ShapeDtypeStruct((halo, d_rnn), F32),
            jax.ShapeDtypeStruct((nb, d_rnn), F32),
        ],
        scratch_shapes=[
            pltpu.VMEM((halo + rows, d_rnn), F32),
            pltpu.VMEM((rows, d_rnn), BF16),
            pltpu.VMEM((nb, d_rnn), F32),
        ],
        compiler_params=pltpu.CompilerParams(
            dimension_semantics=("arbitrary",), vmem_limit_bytes=VMEM_LIMIT_BYTES),
        name="rnn_layer_reset" if reset else "rnn_layer_state",
    )(x, conv0, h0, norm_g.reshape(1, d), w_in, conv_w, conv_b.reshape(1, d_rnn),
      wg, bg, lam.reshape(1, d_rnn), w_out.astype(BF16))


def _group_sum_matrices(width):
    groups = width // HEAD_DIM
    assert groups <= LANE
    gather = np.zeros((width, LANE), np.float32)
    gather[np.arange(width), np.arange(width) // HEAD_DIM] = 1.0
    expand = np.concatenate([gather.T, gather.T], axis=0)
    return jnp.asarray(gather, BF16), jnp.asarray(expand, BF16)


def _group_rms_scale(x, gather, expand):
    ssum = jnp.dot((x * x).astype(BF16), gather, preferred_element_type=F32)
    inv = lax.rsqrt(ssum * (1.0 / HEAD_DIM) + EPS)
    hi = inv.astype(BF16)
    lo = (inv - hi.astype(F32)).astype(BF16)
    return jnp.dot(jnp.concatenate([hi, lo], axis=1), expand, preferred_element_type=F32)


def _rope_slab(xh, cos_t, sin_t):
    lane = lax.broadcasted_iota(jnp.int32, xh.shape, 1)
    first_half = (lane % HEAD_DIM) < (HEAD_DIM // 2)
    partner = jnp.where(first_half, pltpu.roll(xh, LANE - HEAD_DIM // 2, 1),
                        pltpu.roll(xh, HEAD_DIM // 2, 1))
    return xh * cos_t + partner * sin_t


def _proj_kernel(x_ref, gkv_ref, gb_ref, kvw_ref, bw_ref, gk_ref, gq_ref, cos_ref, sin_ref,
                 gather_ref, expand_ref, *out_refs, prompt, kt):
    if prompt:
        k_ref, v_ref, khm_ref, vt_ref, qt_ref, sg_ref = out_refs
    else:
        k_ref, v_ref, q_ref, sg_ref = out_refs
    rows = x_ref.shape[1]
    w = gk_ref.shape[1]
    nh = w // V_DIM
    sub = kt if prompt else rows
    gather, expand = gather_ref[...], expand_ref[...]
    if prompt:
        pad_row = lax.broadcasted_iota(jnp.int32, (BF16_ROWS, kt), 0)
        ones_row = jnp.where(pad_row == 0, 1.0, 0.0).astype(BF16)
    for c in range(rows // sub):
        rs = slice(c * sub, (c + 1) * sub)
        x = x_ref[0, rs, :]
        ms = jnp.mean(x * x, axis=-1, keepdims=True)
        xh = x * lax.rsqrt(ms + EPS)
        kv = jnp.dot((xh * gkv_ref[...]).astype(BF16), kvw_ref[...], preferred_element_type=F32)
        qg = jnp.dot((xh * gb_ref[...]).astype(BF16), bw_ref[...], preferred_element_type=F32)
        k_raw, v = kv[:, :w], kv[:, w:]
        q_raw, half_gate = qg[:, :w], qg[:, w:]
        kn = k_raw * _group_rms_scale(k_raw, gather, expand) * gk_ref[...]
        qn = q_raw * _group_rms_scale(q_raw, gather, expand) * gq_ref[...]
        sg = half_gate * (jnp.tanh(half_gate) + 1.0)
        cos_t, sin_t = cos_ref[rs, :], sin_ref[rs, :]
        if not prompt:
            sg_ref[0] = sg
        for h in range(nh):
            cs = slice(h * V_DIM, (h + 1) * V_DIM)
            kh = _rope_slab(kn[:, cs], cos_t, sin_t)
            qh = _rope_slab(qn[:, cs], cos_t, sin_t)
            k_ref[0, pl.ds(c * sub * nh + h, sub, stride=nh), :] = kh
            v_ref[0, pl.ds(c * sub * nh + h, sub, stride=nh), :] = v[:, cs]
            if prompt:
                khm_ref[0, h, rs, :] = kh.astype(BF16)
                sg_ref[0, h, rs, :] = sg[:, cs].astype(BF16)
                vt_ref[0, h, c, 0:V_DIM, :] = v[:, cs].T.astype(BF16)
                vt_ref[0, h, c, V_DIM:V_DIM + BF16_ROWS, :] = ones_row
                qt_ref[0, h, c] = qh.T.astype(BF16)
            else:
                q_ref[0, :, cs] = qh


def _rope_tables(pos):
    half = HEAD_DIM // 2
    inv = ROPE_THETA ** (-jnp.arange(half, dtype=F32) / half)
    ang = pos.astype(F32)[:, None] * inv[None, :]
    cos, sin = jnp.cos(ang), jnp.sin(ang)
    reps = LANE // HEAD_DIM
    return (jnp.tile(jnp.concatenate([cos, cos], axis=1), (1, reps)),
            jnp.tile(jnp.concatenate([-sin, sin], axis=1), (1, reps)))


def _proj(x, pos, kv_norm, b_norm, kv_w, b_w_in, k_norm, q_norm, *, prompt, rows, kt):
    nb, t, d = x.shape
    w = kv_w.shape[1] // 2
    nh = w // V_DIM
    assert t % rows == 0 and (not prompt or rows % kt == 0)
    cos_t, sin_t = _rope_tables(pos)
    gather, expand = _group_sum_matrices(w)
    reps = w // HEAD_DIM
    gk = jnp.tile(k_norm.astype(F32), reps).reshape(1, w)
    gq = (jnp.tile(q_norm.astype(F32), reps) * (HEAD_DIM ** -0.5 * LOG2E)).reshape(1, w)
    b_w_half = (b_w_in * jnp.where(jnp.arange(2 * w) >= w, 0.5, 1.0)).astype(BF16)

    full = lambda shape: pl.BlockSpec(shape, lambda b, i: (0,) * len(shape),
                                      pipeline_mode=pl.Buffered(1))
    tok = pl.BlockSpec((1, rows, w), lambda b, i: (b, i, 0))
    tok_heads = pl.BlockSpec((1, rows * nh, V_DIM), lambda b, i: (b, i, 0))
    out_specs = [tok_heads, tok_heads]
    out_shape = [jax.ShapeDtypeStruct((nb, t * nh, V_DIM), F32)] * 2
    if prompt:
        nc = rows // kt
        hm = pl.BlockSpec((1, nh, rows, V_DIM), lambda b, i: (b, 0, i, 0))
        tr = lambda r: pl.BlockSpec((1, nh, nc, r, kt), lambda b, i: (b, 0, i, 0, 0))
        out_specs += [hm, tr(V_DIM + BF16_ROWS), tr(V_DIM), hm]
        out_shape += [jax.ShapeDtypeStruct((nb, nh, t, V_DIM), BF16),
                      jax.ShapeDtypeStruct((nb, nh, t // kt, V_DIM + BF16_ROWS, kt), BF16),
                      jax.ShapeDtypeStruct((nb, nh, t // kt, V_DIM, kt), BF16),
                      jax.ShapeDtypeStruct((nb, nh, t, V_DIM), BF16)]
    else:
        out_specs += [tok, tok]
        out_shape += [jax.ShapeDtypeStruct((nb, t, w), F32)] * 2
    return pl.pallas_call(
        functools.partial(_proj_kernel, prompt=prompt, kt=kt),
        grid=(nb, t // rows),
        in_specs=[
            pl.BlockSpec((1, rows, d), lambda b, i: (b, i, 0)),
            full((1, d)), full((1, d)), full((d, 2 * w)), full((d, 2 * w)),
            full((1, w)), full((1, w)),
            pl.BlockSpec((rows, LANE), lambda b, i: (i, 0)),
            pl.BlockSpec((rows, LANE), lambda b, i: (i, 0)),
            full((w, LANE)), full((2 * LANE, w)),
        ],
        out_specs=out_specs,
        out_shape=out_shape,
        compiler_params=pltpu.CompilerParams(
            dimension_semantics=("parallel", "parallel"), vmem_limit_bytes=VMEM_LIMIT_PROJ_BYTES),
        name="proj_prompt" if prompt else "proj_sample",
    )(x, kv_norm.reshape(1, d), b_norm.reshape(1, d), kv_w.astype(BF16), b_w_half,
      gk, gq, cos_t, sin_t, gather, expand)


def _diff_lambda(lq1, lk1, lq2, lk2, lam_init):
    return (jnp.exp(jnp.sum(lq1 * lk1, axis=-1, keepdims=True))
            - jnp.exp(jnp.sum(lq2 * lk2, axis=-1, keepdims=True)) + lam_init)


def _prompt_attn_kernel(qt_ref, khm_ref, vt_ref, sg_ref, x_ref, w_out_ref, hg_ref,
                        lq1_ref, lk1_ref, lq2_ref, lk2_ref, gk_ref, o_ref,
                        qm_s, shift_s, acc_s, p_s, oh_s, *, lam_init):
    nh, vd, tq = qt_ref.shape[1], qt_ref.shape[3], qt_ref.shape[4]
    qi = pl.program_id(1)
    chains = [(h, c) for h in range(nh) for c in range(2)]

    key_norm_bound = (math.sqrt(HEAD_DIM) * KEY_NORM_SLACK
                      * jnp.max(jnp.abs(gk_ref[...]), axis=-1, keepdims=True))
    feat = lax.broadcasted_iota(jnp.int32, (vd, tq), 0)
    for h in range(nh):
        qt = qt_ref[0, h, 0]
        zero = jnp.zeros_like(qt)
        for c in range(2):
            qm = jnp.where((feat >= HEAD_DIM) if c else (feat < HEAD_DIM), qt, zero)
            qm_s[h, c] = qm
            qf = qm.astype(F32)
            shift_s[h, c] = jnp.sqrt(jnp.sum(qf * qf, axis=0, keepdims=True)) * key_norm_bound

    def visible_mask():
        krow = lax.broadcasted_iota(jnp.int32, (tq, tq), 0)
        qcol = lax.broadcasted_iota(jnp.int32, (tq, tq), 1)
        return (krow // CHUNK) <= (qcol // CHUNK)

    def key_tile(h, kt):
        return khm_ref[0, h, pl.ds(pl.multiple_of(kt * tq, tq), tq), :]

    def epilogue():
        lam = _diff_lambda(lq1_ref[...], lk1_ref[...], lq2_ref[...], lk2_ref[...], lam_init)
        for h in range(nh):
            w1 = 1.0 / acc_s[h, 0, vd:vd + 1, :]
            w2 = lam / acc_s[h, 1, vd:vd + 1, :]
            ot = acc_s[h, 0, 0:vd, :] * w1 - acc_s[h, 1, 0:vd, :] * w2
            ms = jnp.mean(ot * ot, axis=0, keepdims=True)
            on = (ot * lax.rsqrt(ms + EPS)) * hg_ref[...]
            oh_s[:, h * vd:(h + 1) * vd] = (on.T * sg_ref[0, h].astype(F32)).astype(BF16)
        o_ref[0] = x_ref[0] + jnp.dot(oh_s[...], w_out_ref[...], preferred_element_type=F32)

    def tile_update(kt, masked):
        if masked:
            visible = visible_mask()
        for h, c in chains:
            s = jnp.dot(key_tile(h, kt), qm_s[h, c], preferred_element_type=F32)
            p = jnp.exp2(s - shift_s[h, c])
            if masked:
                p = jnp.where(visible, p, 0.0)
            p_s[h, c] = p.astype(BF16)
        for h, c in chains:
            acc_s[h, c] += jnp.dot(vt_ref[0, h, kt], p_s[h, c], preferred_element_type=F32)

    def group_body(n, i, carry):
        for j in range(n):
            tile_update(n * i + j, False)
        return carry

    acc_s[...] = jnp.zeros(acc_s.shape, F32)
    lax.fori_loop(0, qi // 4, functools.partial(group_body, 4), 0)

    @pl.when(qi % 4 >= 2)
    def _():
        group_body(2, 2 * (qi // 4), 0)

    @pl.when(qi % 2 == 1)
    def _():
        tile_update(qi - 1, False)

    tile_update(qi, True)
    epilogue()

    denom_min = jnp.min(acc_s[:, :, vd:vd + 1, :])

    @pl.when(jnp.logical_not(denom_min >= DENOM_FLOOR))
    def _():
        visible = visible_mask()

        def head_body(h, carry):
            def tile_step(kt, state, masked):
                new = []
                for c, (m, acc) in enumerate(state):
                    s = jnp.dot(key_tile(h, kt), qm_s[h, c], preferred_element_type=F32)
                    if masked:
                        s = jnp.where(visible, s, NEG_BIG)
                    m_new = jnp.maximum(m, jnp.max(s, axis=0, keepdims=True))
                    p = jnp.exp2(s - m_new).astype(BF16)
                    acc_new = jnp.exp2(m - m_new) * acc + jnp.dot(
                        vt_ref[0, h, kt], p, preferred_element_type=F32)
                    new.append((m_new, acc_new))
                return tuple(new)

            init = tuple((jnp.full((1, tq), NEG_BIG, F32), jnp.zeros(acc_s.shape[2:], F32))
                         for _ in range(2))
            state = lax.fori_loop(0, qi, lambda kt, st: tile_step(kt, st, False), init)
            state = tile_step(qi, state, True)
            for c in range(2):
                acc_s[h, c] = state[c][1]
            return carry

        lax.fori_loop(0, nh, head_body, 0)
        epilogue()


def _prompt_attn(qt, khm, vt, sg, x, w_out, head_g, lq1, lk1, lq2, lk2, k_norm, *, lam_init):
    nb, nh, nq, vd, tq = qt.shape
    vde = vt.shape[3]
    t = khm.shape[2]
    d = x.shape[-1]
    w = nh * vd
    hg = jnp.broadcast_to((head_g.astype(F32) * (1.0 - lam_init))[:, None], (vd, tq))
    full = lambda shape: pl.BlockSpec(shape, lambda b, i: (0,) * len(shape))
    row = lambda a: a.reshape(1, -1).astype(F32)
    return pl.pallas_call(
        functools.partial(_prompt_attn_kernel, lam_init=lam_init),
        grid=(nb, nq),
        in_specs=[
            pl.BlockSpec((1, nh, 1, vd, tq), lambda b, i: (b, 0, i, 0, 0)),
            pl.BlockSpec((1, nh, t, vd), lambda b, i: (b, 0, 0, 0)),
            pl.BlockSpec((1, nh, nq, vde, tq), lambda b, i: (b, 0, 0, 0, 0)),
            pl.BlockSpec((1, nh, tq, vd), lambda b, i: (b, 0, i, 0)),
            pl.BlockSpec((1, tq, d), lambda b, i: (b, i, 0)),
            full((w, d)), full((vd, tq)),
            full((1, HEAD_DIM)), full((1, HEAD_DIM)), full((1, HEAD_DIM)), full((1, HEAD_DIM)),
            full((1, HEAD_DIM)),
        ],
        out_specs=pl.BlockSpec((1, tq, d), lambda b, i: (b, i, 0)),
        out_shape=jax.ShapeDtypeStruct((nb, t, d), F32),
        scratch_shapes=[
            pltpu.VMEM((nh, 2, vd, tq), BF16),
            pltpu.VMEM((nh, 2, 1, tq), F32),
            pltpu.VMEM((nh, 2, vde, tq), F32),
            pltpu.VMEM((nh, 2, tq, tq), BF16),
            pltpu.VMEM((tq, w), BF16),
        ],
        compiler_params=pltpu.CompilerParams(
            dimension_semantics=("parallel", "arbitrary"), vmem_limit_bytes=VMEM_LIMIT_BYTES),
        name="attn_prompt",
    )(qt, khm, vt, sg, x, w_out.astype(BF16), hg, row(lq1), row(lk1), row(lq2), row(lk2),
      row(k_norm))


def _sample_attn_kernel(q_ref, kn_ref, vn_ref, ck_ref, cv_ref, sg_ref, x_ref, w_out_ref, hg_ref,
                        lq1_ref, lk1_ref, lq2_ref, lk2_ref, o_ref, so_s, sn_s, po_s, pn_s, l_s,
                        *, lam_init):
    q = q_ref[0]
    tq, w = q.shape
    nh = w // V_DIM
    past = ck_ref.shape[1] // nh
    lam = _diff_lambda(lq1_ref[...], lk1_ref[...], lq2_ref[...], lk2_ref[...], lam_init)
    lane = lax.broadcasted_iota(jnp.int32, (tq, V_DIM), 1)
    contract_last = (((1,), (1,)), ((), ()))
    for h in range(nh):
        qh = q[:, h * V_DIM:(h + 1) * V_DIM]
        ck = ck_ref[0, pl.ds(h, past, stride=nh), :].astype(BF16)
        kn = kn_ref[0, pl.ds(h, tq, stride=nh), :].astype(BF16)
        q2 = jnp.concatenate([jnp.where(lane < HEAD_DIM, qh, 0.0),
                              jnp.where(lane >= HEAD_DIM, qh, 0.0)], axis=0).astype(BF16)
        so_s[h] = lax.dot_general(q2, ck, contract_last, preferred_element_type=F32)
        sn_s[h] = lax.dot_general(q2, kn, contract_last, preferred_element_type=F32)
    for h in range(nh):
        s_old, s_new = so_s[h], sn_s[h]
        m = jnp.maximum(jnp.max(s_old, axis=-1, keepdims=True),
                        jnp.max(s_new, axis=-1, keepdims=True))
        p_old = jnp.exp2(s_old - m)
        p_new = jnp.exp2(s_new - m)
        l_s[h] = jnp.sum(p_old, axis=-1, keepdims=True) + jnp.sum(p_new, axis=-1, keepdims=True)
        po_s[h] = p_old.astype(BF16)
        pn_s[h] = p_new.astype(BF16)
    outs = []
    for h in range(nh):
        cv = cv_ref[0, pl.ds(h, past, stride=nh), :].astype(BF16)
        vn = vn_ref[0, pl.ds(h, tq, stride=nh), :].astype(BF16)
        pv = (jnp.dot(po_s[h], cv, preferred_element_type=F32)
              + jnp.dot(pn_s[h], vn, preferred_element_type=F32)) / l_s[h]
        oh = pv[0:tq] - lam * pv[tq:2 * tq]
        ms = jnp.mean(oh * oh, axis=-1, keepdims=True)
        outs.append(oh * lax.rsqrt(ms + EPS) * hg_ref[...] * (1.0 - lam_init))
    o = (jnp.concatenate(outs, axis=1) * sg_ref[0]).astype(BF16)
    o_ref[0] = x_ref[0] + jnp.dot(o, w_out_ref[...], preferred_element_type=F32)


def _sample_attn(q, k_new, v_new, cache_k, cache_v, sg, x, w_out, head_g, lq1, lk1, lq2, lk2, *, lam_init):
    nb, tq, w = q.shape
    past = cache_k.shape[1]
    d = x.shape[-1]
    full = lambda shape: pl.BlockSpec(shape, lambda b: (0,) * len(shape))
    tok = lambda width: pl.BlockSpec((1, tq, width), lambda b: (b, 0, 0))
    nh = w // V_DIM
    old = pl.BlockSpec((1, past * nh, V_DIM), lambda b: (b, 0, 0))
    new = pl.BlockSpec((1, tq * nh, V_DIM), lambda b: (b, 0, 0))
    row = lambda a: a.reshape(1, -1).astype(F32)
    return pl.pallas_call(
        functools.partial(_sample_attn_kernel, lam_init=lam_init),
        grid=(nb,),
        in_specs=[tok(w), new, new, old, old, tok(w), tok(d),
                  full((w, d)), full((1, V_DIM)),
                  full((1, HEAD_DIM)), full((1, HEAD_DIM)), full((1, HEAD_DIM)), full((1, HEAD_DIM))],
        out_specs=tok(d),
        out_shape=jax.ShapeDtypeStruct((nb, tq, d), F32),
        scratch_shapes=[
            pltpu.VMEM((nh, 2 * tq, past), F32), pltpu.VMEM((nh, 2 * tq, tq), F32),
            pltpu.VMEM((nh, 2 * tq, past), BF16), pltpu.VMEM((nh, 2 * tq, tq), BF16),
            pltpu.VMEM((nh, 2 * tq, 1), F32),
        ],
        compiler_params=pltpu.CompilerParams(
            dimension_semantics=("parallel",), vmem_limit_bytes=VMEM_LIMIT_BYTES),
        name="attn_sample",
    )(q, k_new, v_new, cache_k.reshape(nb, past * nh, V_DIM), cache_v.reshape(nb, past * nh, V_DIM), sg, x,
      w_out.astype(BF16), row(head_g), row(lq1), row(lk1), row(lq2), row(lk2))


def _pick_tile(n, target):
    tile = min(n, target)
    while n % tile:
        tile //= 2
    return tile


def kernel(x_prompt, x_sample, state_conv, state_h, cache_k, cache_v, a_norm, a_w_in, a_conv_w, a_conv_b, a_gate_r_w, a_gate_r_b, a_gate_i_w, a_gate_i_b, a_lambda, a_w_out, kv_norm, kv_w, k_norm, b_norm, b_w_in, b_q_norm, b_lambda_q1, b_lambda_k1, b_lambda_q2, b_lambda_k2, b_head_norm, b_w_out):
    n_a, n_b = a_norm.shape[0], b_norm.shape[0]
    assert n_a == 1 and n_b == 1, "one recurrent layer followed by one attention layer"
    nb, t, d = x_prompt.shape
    nbs, ts, _ = x_sample.shape
    d_rnn = a_lambda.shape[-1]
    w = kv_w.shape[1] // 2
    nh = w // V_DIM
    past = cache_k.shape[1]
    halo = CONV_W - 1
    lam_init = 0.8 - 0.6 * math.exp(-0.3 * n_a)
    a_args = (a_norm[0], a_w_in[0], a_conv_w[0], a_conv_b[0], a_gate_r_w[0], a_gate_r_b[0],
              a_gate_i_w[0], a_gate_i_b[0], a_lambda[0], a_w_out[0])
    lam_args = (b_lambda_q1[0], b_lambda_k1[0], b_lambda_q2[0], b_lambda_k2[0])

    def conv_state_out(c, n):
        return jnp.swapaxes(c.reshape(halo, n, d_rnn), 0, 1)[None]

    x1, conv_p, h_p = _rnn_layer(
        x_prompt, jnp.zeros((halo * nb, d_rnn), F32), jnp.zeros((nb, d_rnn), F32), *a_args,
        reset=True, tt=_pick_tile(t, 64))
    tq = _pick_tile(t, 256)
    k_p, v_p, khm, vt, qt, sg = _proj(
        x1, jnp.arange(t, dtype=jnp.int32), kv_norm, b_norm[0], kv_w, b_w_in[0], k_norm, b_q_norm[0],
        prompt=True, rows=_pick_tile(t, 1024), kt=tq)
    y_p = _prompt_attn(qt, khm, vt, sg, x1, b_w_out[0], b_head_norm[0], *lam_args, k_norm,
                       lam_init=lam_init)

    conv0 = jnp.swapaxes(state_conv[0], 0, 1).reshape(halo * nbs, d_rnn)
    x1s, conv_s, h_s = _rnn_layer(x_sample, conv0, state_h[0], *a_args, reset=False, tt=ts)
    k_s, v_s, q_s, sg_s = _proj(
        x1s.reshape(1, nbs * ts, d), jnp.tile(past + jnp.arange(ts, dtype=jnp.int32), nbs),
        kv_norm, b_norm[0], kv_w, b_w_in[0], k_norm, b_q_norm[0], prompt=False, rows=nbs * ts, kt=ts)
    k_s, v_s = k_s.reshape(nbs, ts * nh, V_DIM), v_s.reshape(nbs, ts * nh, V_DIM)
    q_s, sg_s = q_s.reshape(nbs, ts, w), sg_s.reshape(nbs, ts, w)
    y_s = _sample_attn(q_s, k_s, v_s, cache_k, cache_v, sg_s, x1s, b_w_out[0], b_head_norm[0],
                       *lam_args, lam_init=lam_init)

    return (y_p, y_s,
            conv_state_out(conv_p, nb), h_p[None],
            k_p.reshape(nb, t, nh, V_DIM), v_p.reshape(nb, t, nh, V_DIM),
            conv_state_out(conv_s, nbs), h_s[None],
            k_s.reshape(nbs, ts, nh, V_DIM), v_s.reshape(nbs, ts, nh, V_DIM))
```

```python
import functools
import math

import numpy as np
import jax
import jax.numpy as jnp
from jax import lax
from jax.experimental import pallas as pl
from jax.experimental.pallas import tpu as pltpu

LANE = 128
SUBLANE = 8
VMEM_LIMIT_BYTES = 56 * 1024 * 1024
VMEM_LIMIT_PROJ_BYTES = 58 * 1024 * 1024

CHUNK = 64
CONV_W = 4
LRU_C = 8.0
HEAD_DIM = 64
V_DIM = 2 * HEAD_DIM
ROPE_THETA = 10000.0
EPS = 1e-6
LOG2E = 1.4426950408889634
NEG_BIG = -1e30
BF16_ROWS = 16
KEY_NORM_SLACK = 1.0 + 2.0 ** -6
DENOM_FLOOR = 2.0 ** -100

F32 = jnp.float32
BF16 = jnp.bfloat16


def _gate_band_layout(d_rnn, blk):
    windows = []
    for j in range(d_rnn // LANE):
        lo = ((LANE * j) // blk) * blk
        hi = ((LANE * j + LANE - 1) // blk + 1) * blk
        ws = (lo // LANE) * LANE
        windows.append((ws, -(-(hi - ws) // LANE) * LANE))
    assert all(ws + width <= d_rnn for ws, width in windows)
    return tuple(windows)


def _block_diag_dense(w):
    n, blk, _ = w.shape
    return jnp.concatenate(
        [jnp.pad(w[i], ((0, 0), (blk * i, blk * (n - 1 - i)))) for i in range(n)], axis=0)


def _rnn_layer_kernel(x_ref, conv0_ref, h0_ref, g_ref, w_in_ref, cw_ref, cb_ref, wg_ref, bg_ref,
                      lam_ref, w_out_ref, o_ref, conv_o_ref, h_o_ref,
                      ubuf, yg_s, hc, *, reset, windows):
    nb, tt, d = x_ref.shape
    rows = nb * tt
    d_rnn = lam_ref.shape[1]
    halo = (CONV_W - 1) * nb
    step = pl.program_id(0)

    @pl.when(step == 0)
    def _():
        ubuf[0:halo, :] = conv0_ref[...]
        hc[...] = h0_ref[...]

    x = jnp.swapaxes(x_ref[...], 0, 1).reshape(rows, d)
    ms = jnp.mean(x * x, axis=-1, keepdims=True)
    xn = (x * lax.rsqrt(ms + EPS) * g_ref[...]).astype(BF16)
    h = jnp.dot(xn, w_in_ref[...], preferred_element_type=F32)
    ubuf[halo:halo + rows, :] = h[:, :d_rnn]
    half_gate = h[:, d_rnn:]

    u = cb_ref[...] + cw_ref[CONV_W - 1:CONV_W, :] * ubuf[halo:halo + rows, :]
    for j in range(CONV_W - 1):
        u = u + cw_ref[j:j + 1, :] * ubuf[j * nb:j * nb + rows, :]
    ubuf[0:halo, :] = ubuf[rows:rows + halo, :]
    conv_o_ref[...] = ubuf[0:halo, :]

    lam = lam_ref[...]
    softplus_neg = jnp.maximum(-lam, 0.0) + jnp.log1p(jnp.exp(-jnp.abs(lam)))
    c2 = (-0.5 * LOG2E * LRU_C) * softplus_neg

    u_bf = u.astype(BF16)
    for j, (ws, width) in enumerate(windows):
        cs = slice(j * LANE, (j + 1) * LANE)
        ri = jnp.dot(u_bf[:, ws:ws + width], wg_ref[j, 0:width, :],
                     preferred_element_type=F32) + bg_ref[j]
        t_r = jnp.tanh(ri[:, :LANE])
        t_i = jnp.tanh(ri[:, LANE:])
        a = jnp.exp2(c2[:, cs] * t_r + c2[:, cs])
        y = 1.0 - a * a
        mult = jnp.where(y > 0.0, y * lax.rsqrt(y), 0.0)
        if reset:
            head = jnp.where(step == 0, 1.0, mult[0:nb])
            mult = jnp.concatenate([head, mult[nb:]], axis=0)
        b = (mult * u[:, cs]) * (0.5 * t_i + 0.5)
        hcur = hc[:, cs]
        hs = []
        for t in range(tt):
            hcur = a[t * nb:(t + 1) * nb] * hcur + b[t * nb:(t + 1) * nb]
            hs.append(hcur)
        hc[:, cs] = hcur
        hg = half_gate[:, cs]
        yg_s[:, cs] = (jnp.concatenate(hs, axis=0) * (hg * (jnp.tanh(hg) + 1.0))).astype(BF16)

    h_o_ref[...] = hc[...]
    out = x + jnp.dot(yg_s[...], w_out_ref[...], preferred_element_type=F32)
    o_ref[...] = jnp.swapaxes(out.reshape(tt, nb, d), 0, 1)


def _rnn_layer(x, conv0, h0, norm_g, w_in, conv_w, conv_b, wr, br, wi, bi, lam, w_out, *, reset, tt):
    nb, t, d = x.shape
    assert nb == SUBLANE and t % tt == 0 and tt % SUBLANE == 0
    d_rnn = lam.shape[-1]
    n_blk, blk, _ = wr.shape
    windows = _gate_band_layout(d_rnn, blk)
    gate_k = max(width for _, width in windows)
    ncol = d_rnn // LANE
    rows = nb * tt
    halo = (CONV_W - 1) * nb

    dense_r = _block_diag_dense(wr)
    dense_i = _block_diag_dense(wi)
    wg = jnp.stack([
        jnp.pad(jnp.concatenate([dense_r[ws:ws + width, j * LANE:(j + 1) * LANE],
                                 dense_i[ws:ws + width, j * LANE:(j + 1) * LANE]], axis=1),
                ((0, gate_k - width), (0, 0)))
        for j, (ws, width) in enumerate(windows)])
    bg = jnp.concatenate([br.reshape(ncol, 1, LANE), bi.reshape(ncol, 1, LANE)], axis=2)
    wg = (0.5 * wg).astype(BF16)
    bg = 0.5 * bg
    gate_cols = jnp.arange(2 * d_rnn) >= d_rnn
    w_in = (w_in * jnp.where(gate_cols, 0.5, 1.0)).astype(BF16)

    full = lambda shape: pl.BlockSpec(shape, lambda i: (0,) * len(shape))
    kern = functools.partial(_rnn_layer_kernel, reset=reset, windows=windows)
    return pl.pallas_call(
        kern,
        grid=(t // tt,),
        in_specs=[
            pl.BlockSpec((nb, tt, d), lambda i: (0, i, 0)),
            full((halo, d_rnn)), full((nb, d_rnn)), full((1, d)),
            full((d, 2 * d_rnn)), full((CONV_W, d_rnn)), full((1, d_rnn)),
            full((ncol, gate_k, 2 * LANE)), full((ncol, 1, 2 * LANE)), full((1, d_rnn)),
            full((d_rnn, d)),
        ],
        out_specs=[
            pl.BlockSpec((nb, tt, d), lambda i: (0, i, 0)),
            full((halo, d_rnn)), full((nb, d_rnn)),
        ],
        out_shape=[
            jax.ShapeDtypeStruct((nb, t, d), F32),
            jax.ShapeDtypeStruct((halo, d_rnn), F32),
            jax.ShapeDtypeStruct((nb, d_rnn), F32),
        ],
        scratch_shapes=[
            pltpu.VMEM((halo + rows, d_rnn), F32),
            pltpu.VMEM((rows, d_rnn), BF16),
            pltpu.VMEM((nb, d_rnn), F32),
        ],
        compiler_params=pltpu.CompilerParams(
            dimension_semantics=("arbitrary",), vmem_limit_bytes=VMEM_LIMIT_BYTES),
        name="rnn_layer_reset" if reset else "rnn_layer_state",
    )(x, conv0, h0, norm_g.reshape(1, d), w_in, conv_w, conv_b.reshape(1, d_rnn),
      wg, bg, lam.reshape(1, d_rnn), w_out.astype(BF16))


def _group_sum_matrices(width):
    groups = width // HEAD_DIM
    assert groups <= LANE
    gather = np.zeros((width, LANE), np.float32)
    gather[np.arange(width), np.arange(width) // HEAD_DIM] = 1.0
    expand = np.concatenate([gather.T, gather.T], axis=0)
    return jnp.asarray(gather, BF16), jnp.asarray(expand, BF16)


def _group_rms_scale(x, gather, expand):
    ssum = jnp.dot((x * x).astype(BF16), gather, preferred_element_type=F32)
    inv = lax.rsqrt(ssum * (1.0 / HEAD_DIM) + EPS)
    hi = inv.astype(BF16)
    lo = (inv - hi.astype(F32)).astype(BF16)
    return jnp.dot(jnp.concatenate([hi, lo], axis=1), expand, preferred_element_type=F32)


def _rope_slab(xh, cos_t, sin_t):
    lane = lax.broadcasted_iota(jnp.int32, xh.shape, 1)
    first_half = (lane % HEAD_DIM) < (HEAD_DIM // 2)
    partner = jnp.where(first_half, pltpu.roll(xh, LANE - HEAD_DIM // 2, 1),
                        pltpu.roll(xh, HEAD_DIM // 2, 1))
    return xh * cos_t + partner * sin_t


def _proj_kernel(x_ref, gkv_ref, gb_ref, kvw_ref, bw_ref, gk_ref, gq_ref, cos_ref, sin_ref,
                 gather_ref, expand_ref, *out_refs, prompt, kt):
    if prompt:
        k_ref, v_ref, khm_ref, vt_ref, qt_ref, sg_ref = out_refs
    else:
        k_ref, v_ref, q_ref, sg_ref = out_refs
    rows = x_ref.shape[1]
    w = gk_ref.shape[1]
    nh = w // V_DIM
    sub = kt if prompt else rows
    gather, expand = gather_ref[...], expand_ref[...]
    if prompt:
        pad_row = lax.broadcasted_iota(jnp.int32, (BF16_ROWS, kt), 0)
        ones_row = jnp.where(pad_row == 0, 1.0, 0.0).astype(BF16)
    for c in range(rows // sub):
        rs = slice(c * sub, (c + 1) * sub)
        x = x_ref[0, rs, :]
        ms = jnp.mean(x * x, axis=-1, keepdims=True)
        xh = x * lax.rsqrt(ms + EPS)
        kv = jnp.dot((xh * gkv_ref[...]).astype(BF16), kvw_ref[...], preferred_element_type=F32)
        qg = jnp.dot((xh * gb_ref[...]).astype(BF16), bw_ref[...], preferred_element_type=F32)
        k_raw, v = kv[:, :w], kv[:, w:]
        q_raw, half_gate = qg[:, :w], qg[:, w:]
        kn = k_raw * _group_rms_scale(k_raw, gather, expand) * gk_ref[...]
        qn = q_raw * _group_rms_scale(q_raw, gather, expand) * gq_ref[...]
        sg = half_gate * (jnp.tanh(half_gate) + 1.0)
        cos_t, sin_t = cos_ref[rs, :], sin_ref[rs, :]
        if not prompt:
            sg_ref[0] = sg
        for h in range(nh):
            cs = slice(h * V_DIM, (h + 1) * V_DIM)
            kh = _rope_slab(kn[:, cs], cos_t, sin_t)
            qh = _rope_slab(qn[:, cs], cos_t, sin_t)
            k_ref[0, pl.ds(c * sub * nh + h, sub, stride=nh), :] = kh
            v_ref[0, pl.ds(c * sub * nh + h, sub, stride=nh), :] = v[:, cs]
            if prompt:
                khm_ref[0, h, rs, :] = kh.astype(BF16)
                sg_ref[0, h, rs, :] = sg[:, cs].astype(BF16)
                vt_ref[0, h, c, 0:V_DIM, :] = v[:, cs].T.astype(BF16)
                vt_ref[0, h, c, V_DIM:V_DIM + BF16_ROWS, :] = ones_row
                qt_ref[0, h, c] = qh.T.astype(BF16)
            else:
                q_ref[0, :, cs] = qh


def _rope_tables(pos):
    half = HEAD_DIM // 2
    inv = ROPE_THETA ** (-jnp.arange(half, dtype=F32) / half)
    ang = pos.astype(F32)[:, None] * inv[None, :]
    cos, sin = jnp.cos(ang), jnp.sin(ang)
    reps = LANE // HEAD_DIM
    return (jnp.tile(jnp.concatenate([cos, cos], axis=1), (1, reps)),
            jnp.tile(jnp.concatenate([-sin, sin], axis=1), (1, reps)))


def _proj(x, pos, kv_norm, b_norm, kv_w, b_w_in, k_norm, q_norm, *, prompt, rows, kt):
    nb, t, d = x.shape
    w = kv_w.shape[1] // 2
    nh = w // V_DIM
    assert t % rows == 0 and (not prompt or rows % kt == 0)
    cos_t, sin_t = _rope_tables(pos)
    gather, expand = _group_sum_matrices(w)
    reps = w // HEAD_DIM
    gk = jnp.tile(k_norm.astype(F32), reps).reshape(1, w)
    gq = (jnp.tile(q_norm.astype(F32), reps) * (HEAD_DIM ** -0.5 * LOG2E)).reshape(1, w)
    b_w_half = (b_w_in * jnp.where(jnp.arange(2 * w) >= w, 0.5, 1.0)).astype(BF16)

    full = lambda shape: pl.BlockSpec(shape, lambda b, i: (0,) * len(shape),
                                      pipeline_mode=pl.Buffered(1))
    tok = pl.BlockSpec((1, rows, w), lambda b, i: (b, i, 0))
    tok_heads = pl.BlockSpec((1, rows * nh, V_DIM), lambda b, i: (b, i, 0))
    out_specs = [tok_heads, tok_heads]
    out_shape = [jax.ShapeDtypeStruct((nb, t * nh, V_DIM), F32)] * 2
    if prompt:
        nc = rows // kt
        hm = pl.BlockSpec((1, nh, rows, V_DIM), lambda b, i: (b, 0, i, 0))
        tr = lambda r: pl.BlockSpec((1, nh, nc, r, kt), lambda b, i: (b, 0, i, 0, 0))
        out_specs += [hm, tr(V_DIM + BF16_ROWS), tr(V_DIM), hm]
        out_shape += [jax.ShapeDtypeStruct((nb, nh, t, V_DIM), BF16),
                      jax.ShapeDtypeStruct((nb, nh, t // kt, V_DIM + BF16_ROWS, kt), BF16),
                      jax.ShapeDtypeStruct((nb, nh, t // kt, V_DIM, kt), BF16),
                      jax.ShapeDtypeStruct((nb, nh, t, V_DIM), BF16)]
    else:
        out_specs += [tok, tok]
        out_shape += [jax.ShapeDtypeStruct((nb, t, w), F32)] * 2
    return pl.pallas_call(
        functools.partial(_proj_kernel, prompt=prompt, kt=kt),
        grid=(nb, t // rows),
        in_specs=[
            pl.BlockSpec((1, rows, d), lambda b, i: (b, i, 0)),
            full((1, d)), full((1, d)), full((d, 2 * w)), full((d, 2 * w)),
            full((1, w)), full((1, w)),
            pl.BlockSpec((rows, LANE), lambda b, i: (i, 0)),
            pl.BlockSpec((rows, LANE), lambda b, i: (i, 0)),
            full((w, LANE)), full((2 * LANE, w)),
        ],
        out_specs=out_specs,
        out_shape=out_shape,
        compiler_params=pltpu.CompilerParams(
            dimension_semantics=("parallel", "parallel"), vmem_limit_bytes=VMEM_LIMIT_PROJ_BYTES),
        name="proj_prompt" if prompt else "proj_sample",
    )(x, kv_norm.reshape(1, d), b_norm.reshape(1, d), kv_w.astype(BF16), b_w_half,
      gk, gq, cos_t, sin_t, gather, expand)


def _diff_lambda(lq1, lk1, lq2, lk2, lam_init):
    return (jnp.exp(jnp.sum(lq1 * lk1, axis=-1, keepdims=True))
            - jnp.exp(jnp.sum(lq2 * lk2, axis=-1, keepdims=True)) + lam_init)


def _prompt_attn_kernel(qt_ref, khm_ref, vt_ref, sg_ref, x_ref, w_out_ref, hg_ref,
                        lq1_ref, lk1_ref, lq2_ref, lk2_ref, gk_ref, o_ref,
                        qm_s, shift_s, acc_s, p_s, oh_s, *, lam_init):
    nh, vd, tq = qt_ref.shape[1], qt_ref.shape[3], qt_ref.shape[4]
    qi = pl.program_id(1)
    chains = [(h, c) for h in range(nh) for c in range(2)]

    key_norm_bound = (math.sqrt(HEAD_DIM) * KEY_NORM_SLACK
                      * jnp.max(jnp.abs(gk_ref[...]), axis=-1, keepdims=True))
    feat = lax.broadcasted_iota(jnp.int32, (vd, tq), 0)
    for h in range(nh):
        qt = qt_ref[0, h, 0]
        zero = jnp.zeros_like(qt)
        for c in range(2):
            qm = jnp.where((feat >= HEAD_DIM) if c else (feat < HEAD_DIM), qt, zero)
            qm_s[h, c] = qm
            qf = qm.astype(F32)
            shift_s[h, c] = jnp.sqrt(jnp.sum(qf * qf, axis=0, keepdims=True)) * key_norm_bound

    def visible_mask():
        krow = lax.broadcasted_iota(jnp.int32, (tq, tq), 0)
        qcol = lax.broadcasted_iota(jnp.int32, (tq, tq), 1)
        return (krow // CHUNK) <= (qcol // CHUNK)

    def key_tile(h, kt):
        return khm_ref[0, h, pl.ds(pl.multiple_of(kt * tq, tq), tq), :]

    def epilogue():
        lam = _diff_lambda(lq1_ref[...], lk1_ref[...], lq2_ref[...], lk2_ref[...], lam_init)
        for h in range(nh):
            w1 = 1.0 / acc_s[h, 0, vd:vd + 1, :]
            w2 = lam / acc_s[h, 1, vd:vd + 1, :]
            ot = acc_s[h, 0, 0:vd, :] * w1 - acc_s[h, 1, 0:vd, :] * w2
            ms = jnp.mean(ot * ot, axis=0, keepdims=True)
            on = (ot * lax.rsqrt(ms + EPS)) * hg_ref[...]
            oh_s[:, h * vd:(h + 1) * vd] = (on.T * sg_ref[0, h].astype(F32)).astype(BF16)
        o_ref[0] = x_ref[0] + jnp.dot(oh_s[...], w_out_ref[...], preferred_element_type=F32)

    def tile_update(kt, masked):
        if masked:
            visible = visible_mask()
        for h, c in chains:
            s = jnp.dot(key_tile(h, kt), qm_s[h, c], preferred_element_type=F32)
            p = jnp.exp2(s - shift_s[h, c])
            if masked:
                p = jnp.where(visible, p, 0.0)
            p_s[h, c] = p.astype(BF16)
        for h, c in chains:
            acc_s[h, c] += jnp.dot(vt_ref[0, h, kt], p_s[h, c], preferred_element_type=F32)

    def group_body(n, i, carry):
        for j in range(n):
            tile_update(n * i + j, False)
        return carry

    acc_s[...] = jnp.zeros(acc_s.shape, F32)
    lax.fori_loop(0, qi // 4, functools.partial(group_body, 4), 0)

    @pl.when(qi % 4 >= 2)
    def _():
        group_body(2, 2 * (qi // 4), 0)

    @pl.when(qi % 2 == 1)
    def _():
        tile_update(qi - 1, False)

    tile_update(qi, True)
    epilogue()

    denom_min = jnp.min(acc_s[:, :, vd:vd + 1, :])

    @pl.when(jnp.logical_not(denom_min >= DENOM_FLOOR))
    def _():
        visible = visible_mask()

        def head_body(h, carry):
            def tile_step(kt, state, masked):
                new = []
                for c, (m, acc) in enumerate(state):
                    s = jnp.dot(key_tile(h, kt), qm_s[h, c], preferred_element_type=F32)
                    if masked:
                        s = jnp.where(visible, s, NEG_BIG)
                    m_new = jnp.maximum(m, jnp.max(s, axis=0, keepdims=True))
                    p = jnp.exp2(s - m_new).astype(BF16)
                    acc_new = jnp.exp2(m - m_new) * acc + jnp.dot(
                        vt_ref[0, h, kt], p, preferred_element_type=F32)
                    new.append((m_new, acc_new))
                return tuple(new)

            init = tuple((jnp.full((1, tq), NEG_BIG, F32), jnp.zeros(acc_s.shape[2:], F32))
                         for _ in range(2))
            state = lax.fori_loop(0, qi, lambda kt, st: tile_step(kt, st, False), init)
            state = tile_step(qi, state, True)
            for c in range(2):
                acc_s[h, c] = state[c][1]
            return carry

        lax.fori_loop(0, nh, head_body, 0)
        epilogue()


def _prompt_attn(qt, khm, vt, sg, x, w_out, head_g, lq1, lk1, lq2, lk2, k_norm, *, lam_init):
    nb, nh, nq, vd, tq = qt.shape
    vde = vt.shape[3]
    t = khm.shape[2]
    d = x.shape[-1]
    w = nh * vd
    hg = jnp.broadcast_to((head_g.astype(F32) * (1.0 - lam_init))[:, None], (vd, tq))
    full = lambda shape: pl.BlockSpec(shape, lambda b, i: (0,) * len(shape))
    row = lambda a: a.reshape(1, -1).astype(F32)
    return pl.pallas_call(
        functools.partial(_prompt_attn_kernel, lam_init=lam_init),
        grid=(nb, nq),
        in_specs=[
            pl.BlockSpec((1, nh, 1, vd, tq), lambda b, i: (b, 0, i, 0, 0)),
            pl.BlockSpec((1, nh, t, vd), lambda b, i: (b, 0, 0, 0)),
            pl.BlockSpec((1, nh, nq, vde, tq), lambda b, i: (b, 0, 0, 0, 0)),
            pl.BlockSpec((1, nh, tq, vd), lambda b, i: (b, 0, i, 0)),
            pl.BlockSpec((1, tq, d), lambda b, i: (b, i, 0)),
            full((w, d)), full((vd, tq)),
            full((1, HEAD_DIM)), full((1, HEAD_DIM)), full((1, HEAD_DIM)), full((1, HEAD_DIM)),
            full((1, HEAD_DIM)),
        ],
        out_specs=pl.BlockSpec((1, tq, d), lambda b, i: (b, i, 0)),
        out_shape=jax.ShapeDtypeStruct((nb, t, d), F32),
        scratch_shapes=[
            pltpu.VMEM((nh, 2, vd, tq), BF16),
            pltpu.VMEM((nh, 2, 1, tq), F32),
            pltpu.VMEM((nh, 2, vde, tq), F32),
            pltpu.VMEM((nh, 2, tq, tq), BF16),
            pltpu.VMEM((tq, w), BF16),
        ],
        compiler_params=pltpu.CompilerParams(
            dimension_semantics=("parallel", "arbitrary"), vmem_limit_bytes=VMEM_LIMIT_BYTES),
        name="attn_prompt",
    )(qt, khm, vt, sg, x, w_out.astype(BF16), hg, row(lq1), row(lk1), row(lq2), row(lk2),
      row(k_norm))


def _sample_attn_kernel(q_ref, kn_ref, vn_ref, ck_ref, cv_ref, sg_ref, x_ref, w_out_ref, hg_ref,
                        lq1_ref, lk1_ref, lq2_ref, lk2_ref, o_ref, so_s, sn_s, po_s, pn_s, l_s,
                        *, lam_init):
    q = q_ref[0]
    tq, w = q.shape
    nh = w // V_DIM
    past = ck_ref.shape[1] // nh
    lam = _diff_lambda(lq1_ref[...], lk1_ref[...], lq2_ref[...], lk2_ref[...], lam_init)
    lane = lax.broadcasted_iota(jnp.int32, (tq, V_DIM), 1)
    contract_last = (((1,), (1,)), ((), ()))
    for h in range(nh):
        qh = q[:, h * V_DIM:(h + 1) * V_DIM]
        ck = ck_ref[0, pl.ds(h, past, stride=nh), :].astype(BF16)
        kn = kn_ref[0, pl.ds(h, tq, stride=nh), :].astype(BF16)
        q2 = jnp.concatenate([jnp.where(lane < HEAD_DIM, qh, 0.0),
                              jnp.where(lane >= HEAD_DIM, qh, 0.0)], axis=0).astype(BF16)
        so_s[h] = lax.dot_general(q2, ck, contract_last, preferred_element_type=F32)
        sn_s[h] = lax.dot_general(q2, kn, contract_last, preferred_element_type=F32)
    for h in range(nh):
        s_old, s_new = so_s[h], sn_s[h]
        m = jnp.maximum(jnp.max(s_old, axis=-1, keepdims=True),
                        jnp.max(s_new, axis=-1, keepdims=True))
        p_old = jnp.exp2(s_old - m)
        p_new = jnp.exp2(s_new - m)
        l_s[h] = jnp.sum(p_old, axis=-1, keepdims=True) + jnp.sum(p_new, axis=-1, keepdims=True)
        po_s[h] = p_old.astype(BF16)
        pn_s[h] = p_new.astype(BF16)
    outs = []
    for h in range(nh):
        cv = cv_ref[0, pl.ds(h, past, stride=nh), :].astype(BF16)
        vn = vn_ref[0, pl.ds(h, tq, stride=nh), :].astype(BF16)
        pv = (jnp.dot(po_s[h], cv, preferred_element_type=F32)
              + jnp.dot(pn_s[h], vn, preferred_element_type=F32)) / l_s[h]
        oh = pv[0:tq] - lam * pv[tq:2 * tq]
        ms = jnp.mean(oh * oh, axis=-1, keepdims=True)
        outs.append(oh * lax.rsqrt(ms + EPS) * hg_ref[...] * (1.0 - lam_init))
    o = (jnp.concatenate(outs, axis=1) * sg_ref[0]).astype(BF16)
    o_ref[0] = x_ref[0] + jnp.dot(o, w_out_ref[...], preferred_element_type=F32)


def _sample_attn(q, k_new, v_new, cache_k, cache_v, sg, x, w_out, head_g, lq1, lk1, lq2, lk2, *, lam_init):
    nb, tq, w = q.shape
    past = cache_k.shape[1]
    d = x.shape[-1]
    full = lambda shape: pl.BlockSpec(shape, lambda b: (0,) * len(shape))
    tok = lambda width: pl.BlockSpec((1, tq, width), lambda b: (b, 0, 0))
    nh = w // V_DIM
    old = pl.BlockSpec((1, past * nh, V_DIM), lambda b: (b, 0, 0))
    new = pl.BlockSpec((1, tq * nh, V_DIM), lambda b: (b, 0, 0))
    row = lambda a: a.reshape(1, -1).astype(F32)
    return pl.pallas_call(
        functools.partial(_sample_attn_kernel, lam_init=lam_init),
        grid=(nb,),
        in_specs=[tok(w), new, new, old, old, tok(w), tok(d),
                  full((w, d)), full((1, V_DIM)),
                  full((1, HEAD_DIM)), full((1, HEAD_DIM)), full((1, HEAD_DIM)), full((1, HEAD_DIM))],
        out_specs=tok(d),
        out_shape=jax.ShapeDtypeStruct((nb, tq, d), F32),
        scratch_shapes=[
            pltpu.VMEM((nh, 2 * tq, past), F32), pltpu.VMEM((nh, 2 * tq, tq), F32),
            pltpu.VMEM((nh, 2 * tq, past), BF16), pltpu.VMEM((nh, 2 * tq, tq), BF16),
            pltpu.VMEM((nh, 2 * tq, 1), F32),
        ],
        compiler_params=pltpu.CompilerParams(
            dimension_semantics=("parallel",), vmem_limit_bytes=VMEM_LIMIT_BYTES),
        name="attn_sample",
    )(q, k_new, v_new, cache_k.reshape(nb, past * nh, V_DIM), cache_v.reshape(nb, past * nh, V_DIM), sg, x,
      w_out.astype(BF16), row(head_g), row(lq1), row(lk1), row(lq2), row(lk2))


def _pick_tile(n, target):
    tile = min(n, target)
    while n % tile:
        tile //= 2
    return tile


def kernel(x_prompt, x_sample, state_conv, state_h, cache_k, cache_v, a_norm, a_w_in, a_conv_w, a_conv_b, a_gate_r_w, a_gate_r_b, a_gate_i_w, a_gate_i_b, a_lambda, a_w_out, kv_norm, kv_w, k_norm, b_norm, b_w_in, b_q_norm, b_lambda_q1, b_lambda_k1, b_lambda_q2, b_lambda_k2, b_head_norm, b_w_out):
    n_a, n_b = a_norm.shape[0], b_norm.shape[0]
    assert n_a == 1 and n_b == 1, "one recurrent layer followed by one attention layer"
    nb, t, d = x_prompt.shape
    nbs, ts, _ = x_sample.shape
    d_rnn = a_lambda.shape[-1]
    w = kv_w.shape[1] // 2
    nh = w // V_DIM
    past = cache_k.shape[1]
    halo = CONV_W - 1
    lam_init = 0.8 - 0.6 * math.exp(-0.3 * n_a)
    a_args = (a_norm[0], a_w_in[0], a_conv_w[0], a_conv_b[0], a_gate_r_w[0], a_gate_r_b[0],
              a_gate_i_w[0], a_gate_i_b[0], a_lambda[0], a_w_out[0])
    lam_args = (b_lambda_q1[0], b_lambda_k1[0], b_lambda_q2[0], b_lambda_k2[0])

    def conv_state_out(c, n):
        return jnp.swapaxes(c.reshape(halo, n, d_rnn), 0, 1)[None]

    x1, conv_p, h_p = _rnn_layer(
        x_prompt, jnp.zeros((halo * nb, d_rnn), F32), jnp.zeros((nb, d_rnn), F32), *a_args,
        reset=True, tt=_pick_tile(t, 128))
    tq = _pick_tile(t, 256)
    k_p, v_p, khm, vt, qt, sg = _proj(
        x1, jnp.arange(t, dtype=jnp.int32), kv_norm, b_norm[0], kv_w, b_w_in[0], k_norm, b_q_norm[0],
        prompt=True, rows=_pick_tile(t, 1024), kt=tq)
    y_p = _prompt_attn(qt, khm, vt, sg, x1, b_w_out[0], b_head_norm[0], *lam_args, k_norm,
                       lam_init=lam_init)

    conv0 = jnp.swapaxes(state_conv[0], 0, 1).reshape(halo * nbs, d_rnn)
    x1s, conv_s, h_s = _rnn_layer(x_sample, conv0, state_h[0], *a_args, reset=False, tt=ts)
    k_s, v_s, q_s, sg_s = _proj(
        x1s.reshape(1, nbs * ts, d), jnp.tile(past + jnp.arange(ts, dtype=jnp.int32), nbs),
        kv_norm, b_norm[0], kv_w, b_w_in[0], k_norm, b_q_norm[0], prompt=False, rows=nbs * ts, kt=ts)
    k_s, v_s = k_s.reshape(nbs, ts * nh, V_DIM), v_s.reshape(nbs, ts * nh, V_DIM)
    q_s, sg_s = q_s.reshape(nbs, ts, w), sg_s.reshape(nbs, ts, w)
    y_s = _sample_attn(q_s, k_s, v_s, cache_k, cache_v, sg_s, x1s, b_w_out[0], b_head_norm[0],
                       *lam_args, lam_init=lam_init)

    return (y_p, y_s,
            conv_state_out(conv_p, nb), h_p[None],
            k_p.reshape(nb, t, nh, V_DIM), v_p.reshape(nb, t, nh, V_DIM),
            conv_state_out(conv_s, nbs), h_s[None],
            k_s.reshape(nbs, ts, nh, V_DIM), v_s.reshape(nbs, ts, nh, V_DIM))
```

```python
import functools
import math

import numpy as np
import jax
import jax.numpy as jnp
from jax import lax
from jax.experimental import pallas as pl
from jax.experimental.pallas import tpu as pltpu

LANE = 128
SUBLANE = 8
VMEM_LIMIT_BYTES = 56 * 1024 * 1024
VMEM_LIMIT_PROJ_BYTES = 60 * 1024 * 1024

CHUNK = 64
CONV_W = 4
LRU_C = 8.0
HEAD_DIM = 64
V_DIM = 2 * HEAD_DIM
ROPE_THETA = 10000.0
EPS = 1e-6
LOG2E = 1.4426950408889634
NEG_BIG = -1e30
BF16_ROWS = 16
KEY_NORM_SLACK = 1.0 + 2.0 ** -6
DENOM_FLOOR = 2.0 ** -100

F32 = jnp.float32
BF16 = jnp.bfloat16


def _gate_band_layout(d_rnn, blk):
    windows = []
    for j in range(d_rnn // LANE):
        lo = ((LANE * j) // blk) * blk
        hi = ((LANE * j + LANE - 1) // blk + 1) * blk
        ws = (lo // LANE) * LANE
        windows.append((ws, -(-(hi - ws) // LANE) * LANE))
    assert all(ws + width <= d_rnn for ws, width in windows)
    return tuple(windows)


def _block_diag_dense(w):
    n, blk, _ = w.shape
    return jnp.concatenate(
        [jnp.pad(w[i], ((0, 0), (blk * i, blk * (n - 1 - i)))) for i in range(n)], axis=0)


def _rnn_layer_kernel(x_ref, conv0_ref, h0_ref, g_ref, w_in_ref, cw_ref, cb_ref, wg_ref, bg_ref,
                      lam_ref, w_out_ref, o_ref, conv_o_ref, h_o_ref,
                      ubuf, yg_s, hc, *, reset, windows):
    nb, tt, d = x_ref.shape
    rows = nb * tt
    d_rnn = lam_ref.shape[1]
    halo = (CONV_W - 1) * nb
    step = pl.program_id(0)

    @pl.when(step == 0)
    def _():
        ubuf[0:halo, :] = conv0_ref[...]
        hc[...] = h0_ref[...]

    x = jnp.swapaxes(x_ref[...], 0, 1).reshape(rows, d)
    ms = jnp.mean(x * x, axis=-1, keepdims=True)
    xn = (x * lax.rsqrt(ms + EPS) * g_ref[...]).astype(BF16)
    h = jnp.dot(xn, w_in_ref[...], preferred_element_type=F32)
    ubuf[halo:halo + rows, :] = h[:, :d_rnn]
    half_gate = h[:, d_rnn:]

    u = cb_ref[...] + cw_ref[CONV_W - 1:CONV_W, :] * ubuf[halo:halo + rows, :]
    for j in range(CONV_W - 1):
        u = u + cw_ref[j:j + 1, :] * ubuf[j * nb:j * nb + rows, :]
    ubuf[0:halo, :] = ubuf[rows:rows + halo, :]
    conv_o_ref[...] = ubuf[0:halo, :]

    lam = lam_ref[...]
    softplus_neg = jnp.maximum(-lam, 0.0) + jnp.log1p(jnp.exp(-jnp.abs(lam)))
    c2 = (-0.5 * LOG2E * LRU_C) * softplus_neg

    u_bf = u.astype(BF16)
    for j, (ws, width) in enumerate(windows):
        cs = slice(j * LANE, (j + 1) * LANE)
        ri = jnp.dot(u_bf[:, ws:ws + width], wg_ref[j, 0:width, :],
                     preferred_element_type=F32) + bg_ref[j]
        t_r = jnp.tanh(ri[:, :LANE])
        t_i = jnp.tanh(ri[:, LANE:])
        a = jnp.exp2(c2[:, cs] * t_r + c2[:, cs])
        y = 1.0 - a * a
        mult = jnp.where(y > 0.0, y * lax.rsqrt(y), 0.0)
        if reset:
            head = jnp.where(step == 0, 1.0, mult[0:nb])
            mult = jnp.concatenate([head, mult[nb:]], axis=0)
        b = (mult * u[:, cs]) * (0.5 * t_i + 0.5)
        hcur = hc[:, cs]
        hs = []
        for t in range(tt):
            hcur = a[t * nb:(t + 1) * nb] * hcur + b[t * nb:(t + 1) * nb]
            hs.append(hcur)
        hc[:, cs] = hcur
        hg = half_gate[:, cs]
        yg_s[:, cs] = (jnp.concatenate(hs, axis=0) * (hg * (jnp.tanh(hg) + 1.0))).astype(BF16)

    h_o_ref[...] = hc[...]
    out = x + jnp.dot(yg_s[...], w_out_ref[...], preferred_element_type=F32)
    o_ref[...] = jnp.swapaxes(out.reshape(tt, nb, d), 0, 1)


def _rnn_layer(x, conv0, h0, norm_g, w_in, conv_w, conv_b, wr, br, wi, bi, lam, w_out, *, reset, tt):
    nb, t, d = x.shape
    assert nb == SUBLANE and t % tt == 0 and tt % SUBLANE == 0
    d_rnn = lam.shape[-1]
    n_blk, blk, _ = wr.shape
    windows = _gate_band_layout(d_rnn, blk)
    gate_k = max(width for _, width in windows)
    ncol = d_rnn // LANE
    rows = nb * tt
    halo = (CONV_W - 1) * nb

    dense_r = _block_diag_dense(wr)
    dense_i = _block_diag_dense(wi)
    wg = jnp.stack([
        jnp.pad(jnp.concatenate([dense_r[ws:ws + width, j * LANE:(j + 1) * LANE],
                                 dense_i[ws:ws + width, j * LANE:(j + 1) * LANE]], axis=1),
                ((0, gate_k - width), (0, 0)))
        for j, (ws, width) in enumerate(windows)])
    bg = jnp.concatenate([br.reshape(ncol, 1, LANE), bi.reshape(ncol, 1, LANE)], axis=2)
    wg = (0.5 * wg).astype(BF16)
    bg = 0.5 * bg
    gate_cols = jnp.arange(2 * d_rnn) >= d_rnn
    w_in = (w_in * jnp.where(gate_cols, 0.5, 1.0)).astype(BF16)

    full = lambda shape: pl.BlockSpec(shape, lambda i: (0,) * len(shape))
    kern = functools.partial(_rnn_layer_kernel, reset=reset, windows=windows)
    return pl.pallas_call(
        kern,
        grid=(t // tt,),
        in_specs=[
            pl.BlockSpec((nb, tt, d), lambda i: (0, i, 0)),
            full((halo, d_rnn)), full((nb, d_rnn)), full((1, d)),
            full((d, 2 * d_rnn)), full((CONV_W, d_rnn)), full((1, d_rnn)),
            full((ncol, gate_k, 2 * LANE)), full((ncol, 1, 2 * LANE)), full((1, d_rnn)),
            full((d_rnn, d)),
        ],
        out_specs=[
            pl.BlockSpec((nb, tt, d), lambda i: (0, i, 0)),
            full((halo, d_rnn)), full((nb, d_rnn)),
        ],
        out_shape=[
            jax.ShapeDtypeStruct((nb, t, d), F32),
            jax.ShapeDtypeStruct((halo, d_rnn), F32),
            jax.ShapeDtypeStruct((nb, d_rnn), F32),
        ],
        scratch_shapes=[
            pltpu.VMEM((halo + rows, d_rnn), F32),
            pltpu.VMEM((rows, d_rnn), BF16),
            pltpu.VMEM((nb, d_rnn), F32),
        ],
        compiler_params=pltpu.CompilerParams(
            dimension_semantics=("arbitrary",), vmem_limit_bytes=VMEM_LIMIT_BYTES),
        name="rnn_layer_reset" if reset else "rnn_layer_state",
    )(x, conv0, h0, norm_g.reshape(1, d), w_in, conv_w, conv_b.reshape(1, d_rnn),
      wg, bg, lam.reshape(1, d_rnn), w_out.astype(BF16))


def _group_sum_matrices(width):
    groups = width // HEAD_DIM
    assert groups <= LANE
    gather = np.zeros((width, LANE), np.float32)
    gather[np.arange(width), np.arange(width) // HEAD_DIM] = 1.0
    expand = np.concatenate([gather.T, gather.T], axis=0)
    return jnp.asarray(gather, BF16), jnp.asarray(expand, BF16)


def _group_rms_scale(x, gather, expand):
    ssum = jnp.dot((x * x).astype(BF16), gather, preferred_element_type=F32)
    inv = lax.rsqrt(ssum * (1.0 / HEAD_DIM) + EPS)
    hi = inv.astype(BF16)
    lo = (inv - hi.astype(F32)).astype(BF16)
    return jnp.dot(jnp.concatenate([hi, lo], axis=1), expand, preferred_element_type=F32)


def _rope_slab(xh, cos_t, sin_t):
    lane = lax.broadcasted_iota(jnp.int32, xh.shape, 1)
    first_half = (lane % HEAD_DIM) < (HEAD_DIM // 2)
    partner = jnp.where(first_half, pltpu.roll(xh, LANE - HEAD_DIM // 2, 1),
                        pltpu.roll(xh, HEAD_DIM // 2, 1))
    return xh * cos_t + partner * sin_t


def _proj_kernel(x_ref, gkv_ref, gb_ref, kvw_ref, bw_ref, gk_ref, gq_ref, cos_ref, sin_ref,
                 gather_ref, expand_ref, *out_refs, prompt, kt):
    if prompt:
        k_ref, v_ref, khm_ref, vt_ref, qt_ref, sg_ref, shift_ref = out_refs
    else:
        k_ref, v_ref, q_ref, sg_ref = out_refs
    rows = x_ref.shape[1]
    w = gk_ref.shape[1]
    nh = w // V_DIM
    sub = kt if prompt else rows
    gather, expand = gather_ref[...], expand_ref[...]
    if prompt:
        pad_row = lax.broadcasted_iota(jnp.int32, (BF16_ROWS, kt), 0)
        ones_row = jnp.where(pad_row == 0, 1.0, 0.0).astype(BF16)
        key_norm_bound = (math.sqrt(HEAD_DIM) * KEY_NORM_SLACK
                          * jnp.max(jnp.abs(gk_ref[...]), axis=-1, keepdims=True))
    for c in range(rows // sub):
        rs = slice(c * sub, (c + 1) * sub)
        x = x_ref[0, rs, :]
        ms = jnp.mean(x * x, axis=-1, keepdims=True)
        xh = x * lax.rsqrt(ms + EPS)
        kv = jnp.dot((xh * gkv_ref[...]).astype(BF16), kvw_ref[...], preferred_element_type=F32)
        qg = jnp.dot((xh * gb_ref[...]).astype(BF16), bw_ref[...], preferred_element_type=F32)
        k_raw, v = kv[:, :w], kv[:, w:]
        q_raw, half_gate = qg[:, :w], qg[:, w:]
        kn = k_raw * _group_rms_scale(k_raw, gather, expand) * gk_ref[...]
        qn = q_raw * _group_rms_scale(q_raw, gather, expand) * gq_ref[...]
        sg = half_gate * (jnp.tanh(half_gate) + 1.0)
        cos_t, sin_t = cos_ref[rs, :], sin_ref[rs, :]
        if not prompt:
            sg_ref[0] = sg
        for h in range(nh):
            cs = slice(h * V_DIM, (h + 1) * V_DIM)
            kh = _rope_slab(kn[:, cs], cos_t, sin_t)
            qh = _rope_slab(qn[:, cs], cos_t, sin_t)
            k_ref[0, pl.ds(c * sub * nh + h, sub, stride=nh), :] = kh
            v_ref[0, pl.ds(c * sub * nh + h, sub, stride=nh), :] = v[:, cs]
            if prompt:
                khm_ref[0, h, rs, :] = kh.astype(BF16)
                sg_ref[0, h, rs, :] = sg[:, cs].astype(BF16)
                vt_ref[0, h, c, 0:V_DIM, :] = v[:, cs].T.astype(BF16)
                vt_ref[0, h, c, V_DIM:V_DIM + BF16_ROWS, :] = ones_row
                qt = qh.T.astype(BF16)
                qt_ref[0, h, c] = qt
                qf = qt.astype(F32)
                sq = qf * qf
                norms = [jnp.sqrt(jnp.sum(sq[m * HEAD_DIM:(m + 1) * HEAD_DIM, :], axis=0, keepdims=True))
                         for m in range(2)]
                shift_ref[0, h, c] = jnp.concatenate(norms, axis=0) * key_norm_bound
            else:
                q_ref[0, :, cs] = qh


def _rope_tables(pos):
    half = HEAD_DIM // 2
    inv = ROPE_THETA ** (-jnp.arange(half, dtype=F32) / half)
    ang = pos.astype(F32)[:, None] * inv[None, :]
    cos, sin = jnp.cos(ang), jnp.sin(ang)
    reps = LANE // HEAD_DIM
    return (jnp.tile(jnp.concatenate([cos, cos], axis=1), (1, reps)),
            jnp.tile(jnp.concatenate([-sin, sin], axis=1), (1, reps)))


def _proj(x, pos, kv_norm, b_norm, kv_w, b_w_in, k_norm, q_norm, *, prompt, rows, kt):
    nb, t, d = x.shape
    w = kv_w.shape[1] // 2
    nh = w // V_DIM
    assert t % rows == 0 and (not prompt or rows % kt == 0)
    cos_t, sin_t = _rope_tables(pos)
    gather, expand = _group_sum_matrices(w)
    reps = w // HEAD_DIM
    gk = jnp.tile(k_norm.astype(F32), reps).reshape(1, w)
    gq = (jnp.tile(q_norm.astype(F32), reps) * (HEAD_DIM ** -0.5 * LOG2E)).reshape(1, w)
    b_w_half = (b_w_in * jnp.where(jnp.arange(2 * w) >= w, 0.5, 1.0)).astype(BF16)

    full = lambda shape: pl.BlockSpec(shape, lambda b, i: (0,) * len(shape),
                                      pipeline_mode=pl.Buffered(1))
    tok = pl.BlockSpec((1, rows, w), lambda b, i: (b, i, 0))
    tok_heads = pl.BlockSpec((1, rows * nh, V_DIM), lambda b, i: (b, i, 0))
    out_specs = [tok_heads, tok_heads]
    out_shape = [jax.ShapeDtypeStruct((nb, t * nh, V_DIM), F32)] * 2
    if prompt:
        nc = rows // kt
        hm = pl.BlockSpec((1, nh, rows, V_DIM), lambda b, i: (b, 0, i, 0))
        tr = lambda r: pl.BlockSpec((1, nh, nc, r, kt), lambda b, i: (b, 0, i, 0, 0))
        out_specs += [hm, tr(V_DIM + BF16_ROWS), tr(V_DIM), hm, tr(2)]
        out_shape += [jax.ShapeDtypeStruct((nb, nh, t, V_DIM), BF16),
                      jax.ShapeDtypeStruct((nb, nh, t // kt, V_DIM + BF16_ROWS, kt), BF16),
                      jax.ShapeDtypeStruct((nb, nh, t // kt, V_DIM, kt), BF16),
                      jax.ShapeDtypeStruct((nb, nh, t, V_DIM), BF16),
                      jax.ShapeDtypeStruct((nb, nh, t // kt, 2, kt), F32)]
    else:
        out_specs += [tok, tok]
        out_shape += [jax.ShapeDtypeStruct((nb, t, w), F32)] * 2
    return pl.pallas_call(
        functools.partial(_proj_kernel, prompt=prompt, kt=kt),
        grid=(nb, t // rows),
        in_specs=[
            pl.BlockSpec((1, rows, d), lambda b, i: (b, i, 0)),
            full((1, d)), full((1, d)), full((d, 2 * w)), full((d, 2 * w)),
            full((1, w)), full((1, w)),
            pl.BlockSpec((rows, LANE), lambda b, i: (i, 0)),
            pl.BlockSpec((rows, LANE), lambda b, i: (i, 0)),
            full((w, LANE)), full((2 * LANE, w)),
        ],
        out_specs=out_specs,
        out_shape=out_shape,
        compiler_params=pltpu.CompilerParams(
            dimension_semantics=("parallel", "parallel"), vmem_limit_bytes=VMEM_LIMIT_PROJ_BYTES),
        name="proj_prompt" if prompt else "proj_sample",
    )(x, kv_norm.reshape(1, d), b_norm.reshape(1, d), kv_w.astype(BF16), b_w_half,
      gk, gq, cos_t, sin_t, gather, expand)


def _diff_lambda(lq1, lk1, lq2, lk2, lam_init):
    return (jnp.exp(jnp.sum(lq1 * lk1, axis=-1, keepdims=True))
            - jnp.exp(jnp.sum(lq2 * lk2, axis=-1, keepdims=True)) + lam_init)


def _prompt_attn_kernel(qt_ref, khm_ref, vt_ref, sg_ref, x_ref, w_out_ref, hg_ref,
                        lq1_ref, lk1_ref, lq2_ref, lk2_ref, shift_ref, o_ref,
                        qm_s, acc_s, p_s, oh_s, *, lam_init):
    nh, vd, tq = qt_ref.shape[1], qt_ref.shape[3], qt_ref.shape[4]
    qi = pl.program_id(1)
    chains = [(h, c) for h in range(nh) for c in range(2)]

    feat = lax.broadcasted_iota(jnp.int32, (vd, tq), 0)
    for h in range(nh):
        qt = qt_ref[0, h, 0]
        zero = jnp.zeros_like(qt)
        qm_s[h, 0] = jnp.where(feat < HEAD_DIM, qt, zero)
        qm_s[h, 1] = jnp.where(feat < HEAD_DIM, zero, qt)

    def visible_mask():
        krow = lax.broadcasted_iota(jnp.int32, (tq, tq), 0)
        qcol = lax.broadcasted_iota(jnp.int32, (tq, tq), 1)
        return (krow // CHUNK) <= (qcol // CHUNK)

    def key_tile(h, kt):
        return khm_ref[0, h, pl.ds(pl.multiple_of(kt * tq, tq), tq), :]

    def epilogue():
        lam = _diff_lambda(lq1_ref[...], lk1_ref[...], lq2_ref[...], lk2_ref[...], lam_init)
        for h in range(nh):
            w1 = 1.0 / acc_s[h, 0, vd:vd + 1, :]
            w2 = lam / acc_s[h, 1, vd:vd + 1, :]
            ot = acc_s[h, 0, 0:vd, :] * w1 - acc_s[h, 1, 0:vd, :] * w2
            ms = jnp.mean(ot * ot, axis=0, keepdims=True)
            on = (ot * lax.rsqrt(ms + EPS)) * hg_ref[...]
            oh_s[:, h * vd:(h + 1) * vd] = (on.T * sg_ref[0, h].astype(F32)).astype(BF16)
        o_ref[0] = x_ref[0] + jnp.dot(oh_s[...], w_out_ref[...], preferred_element_type=F32)

    def tile_update(kt, masked):
        if masked:
            visible = visible_mask()
        for h, c in chains:
            s = jnp.dot(key_tile(h, kt), qm_s[h, c], preferred_element_type=F32)
            p = jnp.exp2(s - shift_ref[0, h, 0, c:c + 1, :])
            if masked:
                p = jnp.where(visible, p, 0.0)
            p_s[h, c] = p.astype(BF16)
        for h, c in chains:
            acc_s[h, c] += jnp.dot(vt_ref[0, h, kt], p_s[h, c], preferred_element_type=F32)

    def group_body(n, i, carry):
        for j in range(n):
            tile_update(n * i + j, False)
        return carry

    acc_s[...] = jnp.zeros(acc_s.shape, F32)
    lax.fori_loop(0, qi // 4, functools.partial(group_body, 4), 0)

    @pl.when(qi % 4 >= 2)
    def _():
        group_body(2, 2 * (qi // 4), 0)

    @pl.when(qi % 2 == 1)
    def _():
        tile_update(qi - 1, False)

    tile_update(qi, True)
    epilogue()

    denom_min = jnp.min(acc_s[:, :, vd:vd + 1, :])

    @pl.when(jnp.logical_not(denom_min >= DENOM_FLOOR))
    def _():
        visible = visible_mask()

        def head_body(h, carry):
            def tile_step(kt, state, masked):
                new = []
                for c, (m, acc) in enumerate(state):
                    s = jnp.dot(key_tile(h, kt), qm_s[h, c], preferred_element_type=F32)
                    if masked:
                        s = jnp.where(visible, s, NEG_BIG)
                    m_new = jnp.maximum(m, jnp.max(s, axis=0, keepdims=True))
                    p = jnp.exp2(s - m_new).astype(BF16)
                    acc_new = jnp.exp2(m - m_new) * acc + jnp.dot(
                        vt_ref[0, h, kt], p, preferred_element_type=F32)
                    new.append((m_new, acc_new))
                return tuple(new)

            init = tuple((jnp.full((1, tq), NEG_BIG, F32), jnp.zeros(acc_s.shape[2:], F32))
                         for _ in range(2))
            state = lax.fori_loop(0, qi, lambda kt, st: tile_step(kt, st, False), init)
            state = tile_step(qi, state, True)
            for c in range(2):
                acc_s[h, c] = state[c][1]
            return carry

        lax.fori_loop(0, nh, head_body, 0)
        epilogue()


def _prompt_attn(qt, khm, vt, sg, shift, x, w_out, head_g, lq1, lk1, lq2, lk2, *, lam_init):
    nb, nh, nq, vd, tq = qt.shape
    vde = vt.shape[3]
    t = khm.shape[2]
    d = x.shape[-1]
    w = nh * vd
    hg = jnp.broadcast_to((head_g.astype(F32) * (1.0 - lam_init))[:, None], (vd, tq))
    full = lambda shape: pl.BlockSpec(shape, lambda b, i: (0,) * len(shape))
    row = lambda a: a.reshape(1, -1).astype(F32)
    return pl.pallas_call(
        functools.partial(_prompt_attn_kernel, lam_init=lam_init),
        grid=(nb, nq),
        in_specs=[
            pl.BlockSpec((1, nh, 1, vd, tq), lambda b, i: (b, 0, i, 0, 0)),
            pl.BlockSpec((1, nh, t, vd), lambda b, i: (b, 0, 0, 0)),
            pl.BlockSpec((1, nh, nq, vde, tq), lambda b, i: (b, 0, 0, 0, 0)),
            pl.BlockSpec((1, nh, tq, vd), lambda b, i: (b, 0, i, 0)),
            pl.BlockSpec((1, tq, d), lambda b, i: (b, i, 0)),
            full((w, d)), full((vd, tq)),
            full((1, HEAD_DIM)), full((1, HEAD_DIM)), full((1, HEAD_DIM)), full((1, HEAD_DIM)),
            pl.BlockSpec((1, nh, 1, 2, tq), lambda b, i: (b, 0, i, 0, 0)),
        ],
        out_specs=pl.BlockSpec((1, tq, d), lambda b, i: (b, i, 0)),
        out_shape=jax.ShapeDtypeStruct((nb, t, d), F32),
        scratch_shapes=[
            pltpu.VMEM((nh, 2, vd, tq), BF16),
            pltpu.VMEM((nh, 2, vde, tq), F32),
            pltpu.VMEM((nh, 2, tq, tq), BF16),
            pltpu.VMEM((tq, w), BF16),
        ],
        compiler_params=pltpu.CompilerParams(
            dimension_semantics=("parallel", "arbitrary"), vmem_limit_bytes=VMEM_LIMIT_BYTES),
        name="attn_prompt",
    )(qt, khm, vt, sg, x, w_out.astype(BF16), hg, row(lq1), row(lk1), row(lq2), row(lk2), shift)


def _sample_attn_kernel(q_ref, kn_ref, vn_ref, ck_ref, cv_ref, sg_ref, x_ref, w_out_ref, hg_ref,
                        lq1_ref, lk1_ref, lq2_ref, lk2_ref, o_ref, so_s, sn_s, po_s, pn_s, l_s,
                        *, lam_init):
    q = q_ref[0]
    tq, w = q.shape
    nh = w // V_DIM
    past = ck_ref.shape[1] // nh
    lam = _diff_lambda(lq1_ref[...], lk1_ref[...], lq2_ref[...], lk2_ref[...], lam_init)
    lane = lax.broadcasted_iota(jnp.int32, (tq, V_DIM), 1)
    contract_last = (((1,), (1,)), ((), ()))
    for h in range(nh):
        qh = q[:, h * V_DIM:(h + 1) * V_DIM]
        ck = ck_ref[0, pl.ds(h, past, stride=nh), :].astype(BF16)
        kn = kn_ref[0, pl.ds(h, tq, stride=nh), :].astype(BF16)
        q2 = jnp.concatenate([jnp.where(lane < HEAD_DIM, qh, 0.0),
                              jnp.where(lane >= HEAD_DIM, qh, 0.0)], axis=0).astype(BF16)
        so_s[h] = lax.dot_general(q2, ck, contract_last, preferred_element_type=F32)
        sn_s[h] = lax.dot_general(q2, kn, contract_last, preferred_element_type=F32)
    for h in range(nh):
        s_old, s_new = so_s[h], sn_s[h]
        m = jnp.maximum(jnp.max(s_old, axis=-1, keepdims=True),
                        jnp.max(s_new, axis=-1, keepdims=True))
        p_old = jnp.exp2(s_old - m)
        p_new = jnp.exp2(s_new - m)
        l_s[h] = jnp.sum(p_old, axis=-1, keepdims=True) + jnp.sum(p_new, axis=-1, keepdims=True)
        po_s[h] = p_old.astype(BF16)
        pn_s[h] = p_new.astype(BF16)
    outs = []
    for h in range(nh):
        cv = cv_ref[0, pl.ds(h, past, stride=nh), :].astype(BF16)
        vn = vn_ref[0, pl.ds(h, tq, stride=nh), :].astype(BF16)
        pv = (jnp.dot(po_s[h], cv, preferred_element_type=F32)
              + jnp.dot(pn_s[h], vn, preferred_element_type=F32)) / l_s[h]
        oh = pv[0:tq] - lam * pv[tq:2 * tq]
        ms = jnp.mean(oh * oh, axis=-1, keepdims=True)
        outs.append(oh * lax.rsqrt(ms + EPS) * hg_ref[...] * (1.0 - lam_init))
    o = (jnp.concatenate(outs, axis=1) * sg_ref[0]).astype(BF16)
    o_ref[0] = x_ref[0] + jnp.dot(o, w_out_ref[...], preferred_element_type=F32)


def _sample_attn(q, k_new, v_new, cache_k, cache_v, sg, x, w_out, head_g, lq1, lk1, lq2, lk2, *, lam_init):
    nb, tq, w = q.shape
    past = cache_k.shape[1]
    d = x.shape[-1]
    full = lambda shape: pl.BlockSpec(shape, lambda b: (0,) * len(shape))
    tok = lambda width: pl.BlockSpec((1, tq, width), lambda b: (b, 0, 0))
    nh = w // V_DIM
    old = pl.BlockSpec((1, past * nh, V_DIM), lambda b: (b, 0, 0))
    new = pl.BlockSpec((1, tq * nh, V_DIM), lambda b: (b, 0, 0))
    row = lambda a: a.reshape(1, -1).astype(F32)
    return pl.pallas_call(
        functools.partial(_sample_attn_kernel, lam_init=lam_init),
        grid=(nb,),
        in_specs=[tok(w), new, new, old, old, tok(w), tok(d),
                  full((w, d)), full((1, V_DIM)),
                  full((1, HEAD_DIM)), full((1, HEAD_DIM)), full((1, HEAD_DIM)), full((1, HEAD_DIM))],
        out_specs=tok(d),
        out_shape=jax.ShapeDtypeStruct((nb, tq, d), F32),
        scratch_shapes=[
            pltpu.VMEM((nh, 2 * tq, past), F32), pltpu.VMEM((nh, 2 * tq, tq), F32),
            pltpu.VMEM((nh, 2 * tq, past), BF16), pltpu.VMEM((nh, 2 * tq, tq), BF16),
            pltpu.VMEM((nh, 2 * tq, 1), F32),
        ],
        compiler_params=pltpu.CompilerParams(
            dimension_semantics=("parallel",), vmem_limit_bytes=VMEM_LIMIT_BYTES),
        name="attn_sample",
    )(q, k_new, v_new, cache_k.reshape(nb, past * nh, V_DIM), cache_v.reshape(nb, past * nh, V_DIM), sg, x,
      w_out.astype(BF16), row(head_g), row(lq1), row(lk1), row(lq2), row(lk2))


def _pick_tile(n, target):
    tile = min(n, target)
    while n % tile:
        tile //= 2
    return tile


def kernel(x_prompt, x_sample, state_conv, state_h, cache_k, cache_v, a_norm, a_w_in, a_conv_w, a_conv_b, a_gate_r_w, a_gate_r_b, a_gate_i_w, a_gate_i_b, a_lambda, a_w_out, kv_norm, kv_w, k_norm, b_norm, b_w_in, b_q_norm, b_lambda_q1, b_lambda_k1, b_lambda_q2, b_lambda_k2, b_head_norm, b_w_out):
    n_a, n_b = a_norm.shape[0], b_norm.shape[0]
    assert n_a == 1 and n_b == 1, "one recurrent layer followed by one attention layer"
    nb, t, d = x_prompt.shape
    nbs, ts, _ = x_sample.shape
    d_rnn = a_lambda.shape[-1]
    w = kv_w.shape[1] // 2
    nh = w // V_DIM
    past = cache_k.shape[1]
    halo = CONV_W - 1
    lam_init = 0.8 - 0.6 * math.exp(-0.3 * n_a)
    a_args = (a_norm[0], a_w_in[0], a_conv_w[0], a_conv_b[0], a_gate_r_w[0], a_gate_r_b[0],
              a_gate_i_w[0], a_gate_i_b[0], a_lambda[0], a_w_out[0])
    lam_args = (b_lambda_q1[0], b_lambda_k1[0], b_lambda_q2[0], b_lambda_k2[0])

    def conv_state_out(c, n):
        return jnp.swapaxes(c.reshape(halo, n, d_rnn), 0, 1)[None]

    x1, conv_p, h_p = _rnn_layer(
        x_prompt, jnp.zeros((halo * nb, d_rnn), F32), jnp.zeros((nb, d_rnn), F32), *a_args,
        reset=True, tt=_pick_tile(t, 128))
    tq = _pick_tile(t, 256)
    k_p, v_p, khm, vt, qt, sg, shift = _proj(
        x1, jnp.arange(t, dtype=jnp.int32), kv_norm, b_norm[0], kv_w, b_w_in[0], k_norm, b_q_norm[0],
        prompt=True, rows=_pick_tile(t, 1024), kt=tq)
    y_p = _prompt_attn(qt, khm, vt, sg, shift, x1, b_w_out[0], b_head_norm[0], *lam_args,
                       lam_init=lam_init)

    conv0 = jnp.swapaxes(state_conv[0], 0, 1).reshape(halo * nbs, d_rnn)
    x1s, conv_s, h_s = _rnn_layer(x_sample, conv0, state_h[0], *a_args, reset=False, tt=ts)
    k_s, v_s, q_s, sg_s = _proj(
        x1s.reshape(1, nbs * ts, d), jnp.tile(past + jnp.arange(ts, dtype=jnp.int32), nbs),
        kv_norm, b_norm[0], kv_w, b_w_in[0], k_norm, b_q_norm[0], prompt=False, rows=nbs * ts, kt=ts)
    k_s, v_s = k_s.reshape(nbs, ts * nh, V_DIM), v_s.reshape(nbs, ts * nh, V_DIM)
    q_s, sg_s = q_s.reshape(nbs, ts, w), sg_s.reshape(nbs, ts, w)
    y_s = _sample_attn(q_s, k_s, v_s, cache_k, cache_v, sg_s, x1s, b_w_out[0], b_head_norm[0],
                       *lam_args, lam_init=lam_init)

    return (y_p, y_s,
            conv_state_out(conv_p, nb), h_p[None],
            k_p.reshape(nb, t, nh, V_DIM), v_p.reshape(nb, t, nh, V_DIM),
            conv_state_out(conv_s, nbs), h_s[None],
            k_s.reshape(nbs, ts, nh, V_DIM), v_s.reshape(nbs, ts, nh, V_DIM))
```

```python
import functools
import math

import numpy as np
import jax
import jax.numpy as jnp
from jax import lax
from jax.experimental import pallas as pl
from jax.experimental.pallas import tpu as pltpu

LANE = 128
SUBLANE = 8
VMEM_LIMIT_BYTES = 56 * 1024 * 1024
VMEM_LIMIT_PROJ_BYTES = 58 * 1024 * 1024

CHUNK = 64
CONV_W = 4
LRU_C = 8.0
HEAD_DIM = 64
V_DIM = 2 * HEAD_DIM
ROPE_THETA = 10000.0
EPS = 1e-6
LOG2E = 1.4426950408889634
NEG_BIG = -1e30
BF16_ROWS = 16
KEY_NORM_SLACK = 1.0 + 2.0 ** -6
DENOM_FLOOR = 2.0 ** -100

F32 = jnp.float32
BF16 = jnp.bfloat16


def _gate_band_layout(d_rnn, blk):
    windows = []
    for j in range(d_rnn // LANE):
        lo = ((LANE * j) // blk) * blk
        hi = ((LANE * j + LANE - 1) // blk + 1) * blk
        ws = (lo // LANE) * LANE
        windows.append((ws, -(-(hi - ws) // LANE) * LANE))
    assert all(ws + width <= d_rnn for ws, width in windows)
    return tuple(windows)


def _block_diag_dense(w):
    n, blk, _ = w.shape
    return jnp.concatenate(
        [jnp.pad(w[i], ((0, 0), (blk * i, blk * (n - 1 - i)))) for i in range(n)], axis=0)


def _rnn_layer_kernel(x_ref, conv0_ref, h0_ref, g_ref, w_in_ref, cw_ref, cb_ref, wg_ref, bg_ref,
                      lam_ref, w_out_ref, o_ref, conv_o_ref, h_o_ref,
                      ubuf, yg_s, hc, *, reset, windows):
    nb, tt, d = x_ref.shape
    rows = nb * tt
    d_rnn = lam_ref.shape[1]
    halo = (CONV_W - 1) * nb
    step = pl.program_id(0)

    @pl.when(step == 0)
    def _():
        ubuf[0:halo, :] = conv0_ref[...]
        hc[...] = h0_ref[...]

    x = jnp.swapaxes(x_ref[...], 0, 1).reshape(rows, d)
    ms = jnp.mean(x * x, axis=-1, keepdims=True)
    xn = (x * lax.rsqrt(ms + EPS) * g_ref[...]).astype(BF16)
    h = jnp.dot(xn, w_in_ref[...], preferred_element_type=F32)
    ubuf[halo:halo + rows, :] = h[:, :d_rnn]
    half_gate = h[:, d_rnn:]

    u = cb_ref[...] + cw_ref[CONV_W - 1:CONV_W, :] * ubuf[halo:halo + rows, :]
    for j in range(CONV_W - 1):
        u = u + cw_ref[j:j + 1, :] * ubuf[j * nb:j * nb + rows, :]
    ubuf[0:halo, :] = ubuf[rows:rows + halo, :]
    conv_o_ref[...] = ubuf[0:halo, :]

    lam = lam_ref[...]
    softplus_neg = jnp.maximum(-lam, 0.0) + jnp.log1p(jnp.exp(-jnp.abs(lam)))
    c2 = (-0.5 * LOG2E * LRU_C) * softplus_neg

    u_bf = u.astype(BF16)
    for j, (ws, width) in enumerate(windows):
        cs = slice(j * LANE, (j + 1) * LANE)
        ri = jnp.dot(u_bf[:, ws:ws + width], wg_ref[j, 0:width, :],
                     preferred_element_type=F32) + bg_ref[j]
        t_r = jnp.tanh(ri[:, :LANE])
        t_i = jnp.tanh(ri[:, LANE:])
        a = jnp.exp2(c2[:, cs] * t_r + c2[:, cs])
        y = 1.0 - a * a
        mult = jnp.where(y > 0.0, y * lax.rsqrt(y), 0.0)
        if reset:
            head = jnp.where(step == 0, 1.0, mult[0:nb])
            mult = jnp.concatenate([head, mult[nb:]], axis=0)
        b = (mult * u[:, cs]) * (0.5 * t_i + 0.5)
        hcur = hc[:, cs]
        hs = []
        for t in range(tt):
            hcur = a[t * nb:(t + 1) * nb] * hcur + b[t * nb:(t + 1) * nb]
            hs.append(hcur)
        hc[:, cs] = hcur
        hg = half_gate[:, cs]
        yg_s[:, cs] = (jnp.concatenate(hs, axis=0) * (hg * (jnp.tanh(hg) + 1.0))).astype(BF16)

    h_o_ref[...] = hc[...]
    out = x + jnp.dot(yg_s[...], w_out_ref[...], preferred_element_type=F32)
    o_ref[...] = jnp.swapaxes(out.reshape(tt, nb, d), 0, 1)


def _rnn_layer(x, conv0, h0, norm_g, w_in, conv_w, conv_b, wr, br, wi, bi, lam, w_out, *, reset, tt):
    nb, t, d = x.shape
    assert nb == SUBLANE and t % tt == 0 and tt % SUBLANE == 0
    d_rnn = lam.shape[-1]
    n_blk, blk, _ = wr.shape
    windows = _gate_band_layout(d_rnn, blk)
    gate_k = max(width for _, width in windows)
    ncol = d_rnn // LANE
    rows = nb * tt
    halo = (CONV_W - 1) * nb

    dense_r = _block_diag_dense(wr)
    dense_i = _block_diag_dense(wi)
    wg = jnp.stack([
        jnp.pad(jnp.concatenate([dense_r[ws:ws + width, j * LANE:(j + 1) * LANE],
                                 dense_i[ws:ws + width, j * LANE:(j + 1) * LANE]], axis=1),
                ((0, gate_k - width), (0, 0)))
        for j, (ws, width) in enumerate(windows)])
    bg = jnp.concatenate([br.reshape(ncol, 1, LANE), bi.reshape(ncol, 1, LANE)], axis=2)
    wg = (0.5 * wg).astype(BF16)
    bg = 0.5 * bg
    gate_cols = jnp.arange(2 * d_rnn) >= d_rnn
    w_in = (w_in * jnp.where(gate_cols, 0.5, 1.0)).astype(BF16)

    full = lambda shape: pl.BlockSpec(shape, lambda i: (0,) * len(shape))
    kern = functools.partial(_rnn_layer_kernel, reset=reset, windows=windows)
    return pl.pallas_call(
        kern,
        grid=(t // tt,),
        in_specs=[
            pl.BlockSpec((nb, tt, d), lambda i: (0, i, 0)),
            full((halo, d_rnn)), full((nb, d_rnn)), full((1, d)),
            full((d, 2 * d_rnn)), full((CONV_W, d_rnn)), full((1, d_rnn)),
            full((ncol, gate_k, 2 * LANE)), full((ncol, 1, 2 * LANE)), full((1, d_rnn)),
            full((d_rnn, d)),
        ],
        out_specs=[
            pl.BlockSpec((nb, tt, d), lambda i: (0, i, 0)),
            full((halo, d_rnn)), full((nb, d_rnn)),
        ],
        out_shape=[
            jax.ShapeDtypeStruct((nb, t, d), F32),
            jax.ShapeDtypeStruct((halo, d_rnn), F32),
            jax.ShapeDtypeStruct((nb, d_rnn), F32),
        ],
        scratch_shapes=[
            pltpu.VMEM((halo + rows, d_rnn), F32),
            pltpu.VMEM((rows, d_rnn), BF16),
            pltpu.VMEM((nb, d_rnn), F32),
        ],
        compiler_params=pltpu.CompilerParams(
            dimension_semantics=("arbitrary",), vmem_limit_bytes=VMEM_LIMIT_BYTES),
        name="rnn_layer_reset" if reset else "rnn_layer_state",
    )(x, conv0, h0, norm_g.reshape(1, d), w_in, conv_w, conv_b.reshape(1, d_rnn),
      wg, bg, lam.reshape(1, d_rnn), w_out.astype(BF16))


def _group_sum_matrices(width):
    groups = width // HEAD_DIM
    assert groups <= LANE
    gather = np.zeros((width, LANE), np.float32)
    gather[np.arange(width), np.arange(width) // HEAD_DIM] = 1.0
    expand = np.concatenate([gather.T, gather.T], axis=0)
    return jnp.asarray(gather, BF16), jnp.asarray(expand, BF16)


def _group_rms_scale(x, gather, expand):
    ssum = jnp.dot((x * x).astype(BF16), gather, preferred_element_type=F32)
    inv = lax.rsqrt(ssum * (1.0 / HEAD_DIM) + EPS)
    hi = inv.astype(BF16)
    lo = (inv - hi.astype(F32)).astype(BF16)
    return jnp.dot(jnp.concatenate([hi, lo], axis=1), expand, preferred_element_type=F32)


def _rope_slab(xh, cos_t, sin_t):
    lane = lax.broadcasted_iota(jnp.int32, xh.shape, 1)
    first_half = (lane % HEAD_DIM) < (HEAD_DIM // 2)
    partner = jnp.where(first_half, pltpu.roll(xh, LANE - HEAD_DIM // 2, 1),
                        pltpu.roll(xh, HEAD_DIM // 2, 1))
    return xh * cos_t + partner * sin_t


def _proj_kernel(x_ref, gkv_ref, gb_ref, kvw_ref, bw_ref, gk_ref, gq_ref, cos_ref, sin_ref,
                 gather_ref, expand_ref, *out_refs, prompt, kt):
    if prompt:
        k_ref, v_ref, khm_ref, vt_ref, qt_ref, sg_ref = out_refs
    else:
        k_ref, v_ref, q_ref, sg_ref = out_refs
    rows = x_ref.shape[1]
    w = gk_ref.shape[1]
    nh = w // V_DIM
    sub = kt if prompt else rows
    gather, expand = gather_ref[...], expand_ref[...]
    if prompt:
        pad_row = lax.broadcasted_iota(jnp.int32, (BF16_ROWS, kt), 0)
        ones_row = jnp.where(pad_row == 0, 1.0, 0.0).astype(BF16)
    for c in range(rows // sub):
        rs = slice(c * sub, (c + 1) * sub)
        x = x_ref[0, rs, :]
        ms = jnp.mean(x * x, axis=-1, keepdims=True)
        xh = x * lax.rsqrt(ms + EPS)
        kv = jnp.dot((xh * gkv_ref[...]).astype(BF16), kvw_ref[...], preferred_element_type=F32)
        qg = jnp.dot((xh * gb_ref[...]).astype(BF16), bw_ref[...], preferred_element_type=F32)
        k_raw, v = kv[:, :w], kv[:, w:]
        q_raw, half_gate = qg[:, :w], qg[:, w:]
        kn = k_raw * _group_rms_scale(k_raw, gather, expand) * gk_ref[...]
        qn = q_raw * _group_rms_scale(q_raw, gather, expand) * gq_ref[...]
        sg = half_gate * (jnp.tanh(half_gate) + 1.0)
        cos_t, sin_t = cos_ref[rs, :], sin_ref[rs, :]
        if not prompt:
            sg_ref[0] = sg
        for h in range(nh):
            cs = slice(h * V_DIM, (h + 1) * V_DIM)
            kh = _rope_slab(kn[:, cs], cos_t, sin_t)
            qh = _rope_slab(qn[:, cs], cos_t, sin_t)
            k_ref[0, pl.ds(c * sub * nh + h, sub, stride=nh), :] = kh
            v_ref[0, pl.ds(c * sub * nh + h, sub, stride=nh), :] = v[:, cs]
            if prompt:
                khm_ref[0, h, rs, :] = kh.astype(BF16)
                sg_ref[0, h, rs, :] = sg[:, cs].astype(BF16)
                vt_ref[0, h, c, 0:V_DIM, :] = v[:, cs].T.astype(BF16)
                vt_ref[0, h, c, V_DIM:V_DIM + BF16_ROWS, :] = ones_row
                qt_ref[0, h, c] = qh.T.astype(BF16)
            else:
                q_ref[0, :, cs] = qh


def _rope_tables(pos):
    half = HEAD_DIM // 2
    inv = ROPE_THETA ** (-jnp.arange(half, dtype=F32) / half)
    ang = pos.astype(F32)[:, None] * inv[None, :]
    cos, sin = jnp.cos(ang), jnp.sin(ang)
    reps = LANE // HEAD_DIM
    return (jnp.tile(jnp.concatenate([cos, cos], axis=1), (1, reps)),
            jnp.tile(jnp.concatenate([-sin, sin], axis=1), (1, reps)))


def _proj(x, pos, kv_norm, b_norm, kv_w, b_w_in, k_norm, q_norm, *, prompt, rows, kt):
    nb, t, d = x.shape
    w = kv_w.shape[1] // 2
    nh = w // V_DIM
    assert t % rows == 0 and (not prompt or rows % kt == 0)
    cos_t, sin_t = _rope_tables(pos)
    gather, expand = _group_sum_matrices(w)
    reps = w // HEAD_DIM
    gk = jnp.tile(k_norm.astype(F32), reps).reshape(1, w)
    gq = (jnp.tile(q_norm.astype(F32), reps) * (HEAD_DIM ** -0.5 * LOG2E)).reshape(1, w)
    b_w_half = (b_w_in * jnp.where(jnp.arange(2 * w) >= w, 0.5, 1.0)).astype(BF16)

    full = lambda shape: pl.BlockSpec(shape, lambda b, i: (0,) * len(shape),
                                      pipeline_mode=pl.Buffered(1))
    tok = pl.BlockSpec((1, rows, w), lambda b, i: (b, i, 0))
    tok_heads = pl.BlockSpec((1, rows * nh, V_DIM), lambda b, i: (b, i, 0))
    out_specs = [tok_heads, tok_heads]
    out_shape = [jax.ShapeDtypeStruct((nb, t * nh, V_DIM), F32)] * 2
    if prompt:
        nc = rows // kt
        hm = pl.BlockSpec((1, nh, rows, V_DIM), lambda b, i: (b, 0, i, 0))
        tr = lambda r: pl.BlockSpec((1, nh, nc, r, kt), lambda b, i: (b, 0, i, 0, 0))
        out_specs += [hm, tr(V_DIM + BF16_ROWS), tr(V_DIM), hm]
        out_shape += [jax.ShapeDtypeStruct((nb, nh, t, V_DIM), BF16),
                      jax.ShapeDtypeStruct((nb, nh, t // kt, V_DIM + BF16_ROWS, kt), BF16),
                      jax.ShapeDtypeStruct((nb, nh, t // kt, V_DIM, kt), BF16),
                      jax.ShapeDtypeStruct((nb, nh, t, V_DIM), BF16)]
    else:
        out_specs += [tok, tok]
        out_shape += [jax.ShapeDtypeStruct((nb, t, w), F32)] * 2
    return pl.pallas_call(
        functools.partial(_proj_kernel, prompt=prompt, kt=kt),
        grid=(nb, t // rows),
        in_specs=[
            pl.BlockSpec((1, rows, d), lambda b, i: (b, i, 0)),
            full((1, d)), full((1, d)), full((d, 2 * w)), full((d, 2 * w)),
            full((1, w)), full((1, w)),
            pl.BlockSpec((rows, LANE), lambda b, i: (i, 0)),
            pl.BlockSpec((rows, LANE), lambda b, i: (i, 0)),
            full((w, LANE)), full((2 * LANE, w)),
        ],
        out_specs=out_specs,
        out_shape=out_shape,
        compiler_params=pltpu.CompilerParams(
            dimension_semantics=("parallel", "parallel"), vmem_limit_bytes=VMEM_LIMIT_PROJ_BYTES),
        name="proj_prompt" if prompt else "proj_sample",
    )(x, kv_norm.reshape(1, d), b_norm.reshape(1, d), kv_w.astype(BF16), b_w_half,
      gk, gq, cos_t, sin_t, gather, expand)


def _diff_lambda(lq1, lk1, lq2, lk2, lam_init):
    return (jnp.exp(jnp.sum(lq1 * lk1, axis=-1, keepdims=True))
            - jnp.exp(jnp.sum(lq2 * lk2, axis=-1, keepdims=True)) + lam_init)


def _prompt_attn_kernel(qt_ref, khm_ref, vt_ref, sg_ref, x_ref, w_out_ref, hg_ref,
                        lq1_ref, lk1_ref, lq2_ref, lk2_ref, gk_ref, o_ref,
                        qm_s, shift_s, acc_s, p_s, oh_s, *, lam_init):
    nh, vd, tq = qt_ref.shape[1], qt_ref.shape[3], qt_ref.shape[4]
    qi = pl.program_id(1)
    chains = [(h, c) for h in range(nh) for c in range(2)]

    key_norm_bound = (math.sqrt(HEAD_DIM) * KEY_NORM_SLACK
                      * jnp.max(jnp.abs(gk_ref[...]), axis=-1, keepdims=True))
    feat = lax.broadcasted_iota(jnp.int32, (vd, tq), 0)
    for h in range(nh):
        qt = qt_ref[0, h, 0]
        zero = jnp.zeros_like(qt)
        for c in range(2):
            qm = jnp.where((feat >= HEAD_DIM) if c else (feat < HEAD_DIM), qt, zero)
            qm_s[h, c] = qm
            qf = qm.astype(F32)
            shift_s[h, c] = jnp.sqrt(jnp.sum(qf * qf, axis=0, keepdims=True)) * key_norm_bound

    def visible_mask():
        krow = lax.broadcasted_iota(jnp.int32, (tq, tq), 0)
        qcol = lax.broadcasted_iota(jnp.int32, (tq, tq), 1)
        return (krow // CHUNK) <= (qcol // CHUNK)

    def key_tile(h, kt):
        return khm_ref[0, h, pl.ds(pl.multiple_of(kt * tq, tq), tq), :]

    def epilogue():
        lam = _diff_lambda(lq1_ref[...], lk1_ref[...], lq2_ref[...], lk2_ref[...], lam_init)
        for h in range(nh):
            w1 = 1.0 / acc_s[h, 0, vd:vd + 1, :]
            w2 = lam / acc_s[h, 1, vd:vd + 1, :]
            ot = acc_s[h, 0, 0:vd, :] * w1 - acc_s[h, 1, 0:vd, :] * w2
            ms = jnp.mean(ot * ot, axis=0, keepdims=True)
            on = (ot * lax.rsqrt(ms + EPS)) * hg_ref[...]
            oh_s[:, h * vd:(h + 1) * vd] = (on.T * sg_ref[0, h].astype(F32)).astype(BF16)
        o_ref[0] = x_ref[0] + jnp.dot(oh_s[...], w_out_ref[...], preferred_element_type=F32)

    def tile_update(kt, masked):
        if masked:
            visible = visible_mask()
        for h, c in chains:
            s = jnp.dot(key_tile(h, kt), qm_s[h, c], preferred_element_type=F32)
            p = jnp.exp2(s - shift_s[h, c])
            if masked:
                p = jnp.where(visible, p, 0.0)
            p_s[h, c] = p.astype(BF16)
        for h, c in chains:
            acc_s[h, c] += jnp.dot(vt_ref[0, h, kt], p_s[h, c], preferred_element_type=F32)

    def group_body(n, i, carry):
        for j in range(n):
            tile_update(n * i + j, False)
        return carry

    acc_s[...] = jnp.zeros(acc_s.shape, F32)
    lax.fori_loop(0, qi // 8, functools.partial(group_body, 8), 0)

    @pl.when(qi % 8 >= 4)
    def _():
        group_body(4, 2 * (qi // 8), 0)

    @pl.when(qi % 4 >= 2)
    def _():
        group_body(2, 2 * (qi // 4), 0)

    @pl.when(qi % 2 == 1)
    def _():
        tile_update(qi - 1, False)

    tile_update(qi, True)
    epilogue()

    denom_min = jnp.min(acc_s[:, :, vd:vd + 1, :])

    @pl.when(jnp.logical_not(denom_min >= DENOM_FLOOR))
    def _():
        visible = visible_mask()

        def head_body(h, carry):
            def tile_step(kt, state, masked):
                new = []
                for c, (m, acc) in enumerate(state):
                    s = jnp.dot(key_tile(h, kt), qm_s[h, c], preferred_element_type=F32)
                    if masked:
                        s = jnp.where(visible, s, NEG_BIG)
                    m_new = jnp.maximum(m, jnp.max(s, axis=0, keepdims=True))
                    p = jnp.exp2(s - m_new).astype(BF16)
                    acc_new = jnp.exp2(m - m_new) * acc + jnp.dot(
                        vt_ref[0, h, kt], p, preferred_element_type=F32)
                    new.append((m_new, acc_new))
                return tuple(new)

            init = tuple((jnp.full((1, tq), NEG_BIG, F32), jnp.zeros(acc_s.shape[2:], F32))
                         for _ in range(2))
            state = lax.fori_loop(0, qi, lambda kt, st: tile_step(kt, st, False), init)
            state = tile_step(qi, state, True)
            for c in range(2):
                acc_s[h, c] = state[c][1]
            return carry

        lax.fori_loop(0, nh, head_body, 0)
        epilogue()


def _prompt_attn(qt, khm, vt, sg, x, w_out, head_g, lq1, lk1, lq2, lk2, k_norm, *, lam_init):
    nb, nh, nq, vd, tq = qt.shape
    vde = vt.shape[3]
    t = khm.shape[2]
    d = x.shape[-1]
    w = nh * vd
    hg = jnp.broadcast_to((head_g.astype(F32) * (1.0 - lam_init))[:, None], (vd, tq))
    full = lambda shape: pl.BlockSpec(shape, lambda b, i: (0,) * len(shape))
    row = lambda a: a.reshape(1, -1).astype(F32)
    return pl.pallas_call(
        functools.partial(_prompt_attn_kernel, lam_init=lam_init),
        grid=(nb, nq),
        in_specs=[
            pl.BlockSpec((1, nh, 1, vd, tq), lambda b, i: (b, 0, i, 0, 0)),
            pl.BlockSpec((1, nh, t, vd), lambda b, i: (b, 0, 0, 0)),
            pl.BlockSpec((1, nh, nq, vde, tq), lambda b, i: (b, 0, 0, 0, 0)),
            pl.BlockSpec((1, nh, tq, vd), lambda b, i: (b, 0, i, 0)),
            pl.BlockSpec((1, tq, d), lambda b, i: (b, i, 0)),
            full((w, d)), full((vd, tq)),
            full((1, HEAD_DIM)), full((1, HEAD_DIM)), full((1, HEAD_DIM)), full((1, HEAD_DIM)),
            full((1, HEAD_DIM)),
        ],
        out_specs=pl.BlockSpec((1, tq, d), lambda b, i: (b, i, 0)),
        out_shape=jax.ShapeDtypeStruct((nb, t, d), F32),
        scratch_shapes=[
            pltpu.VMEM((nh, 2, vd, tq), BF16),
            pltpu.VMEM((nh, 2, 1, tq), F32),
            pltpu.VMEM((nh, 2, vde, tq), F32),
            pltpu.VMEM((nh, 2, tq, tq), BF16),
            pltpu.VMEM((tq, w), BF16),
        ],
        compiler_params=pltpu.CompilerParams(
            dimension_semantics=("parallel", "arbitrary"), vmem_limit_bytes=VMEM_LIMIT_BYTES),
        name="attn_prompt",
    )(qt, khm, vt, sg, x, w_out.astype(BF16), hg, row(lq1), row(lk1), row(lq2), row(lk2),
      row(k_norm))


def _sample_attn_kernel(q_ref, kn_ref, vn_ref, ck_ref, cv_ref, sg_ref, x_ref, w_out_ref, hg_ref,
                        lq1_ref, lk1_ref, lq2_ref, lk2_ref, o_ref, so_s, sn_s, po_s, pn_s, l_s,
                        *, lam_init):
    q = q_ref[0]
    tq, w = q.shape
    nh = w // V_DIM
    past = ck_ref.shape[1] // nh
    lam = _diff_lambda(lq1_ref[...], lk1_ref[...], lq2_ref[...], lk2_ref[...], lam_init)
    lane = lax.broadcasted_iota(jnp.int32, (tq, V_DIM), 1)
    contract_last = (((1,), (1,)), ((), ()))
    for h in range(nh):
        qh = q[:, h * V_DIM:(h + 1) * V_DIM]
        ck = ck_ref[0, pl.ds(h, past, stride=nh), :].astype(BF16)
        kn = kn_ref[0, pl.ds(h, tq, stride=nh), :].astype(BF16)
        q2 = jnp.concatenate([jnp.where(lane < HEAD_DIM, qh, 0.0),
                              jnp.where(lane >= HEAD_DIM, qh, 0.0)], axis=0).astype(BF16)
        so_s[h] = lax.dot_general(q2, ck, contract_last, preferred_element_type=F32)
        sn_s[h] = lax.dot_general(q2, kn, contract_last, preferred_element_type=F32)
    for h in range(nh):
        s_old, s_new = so_s[h], sn_s[h]
        m = jnp.maximum(jnp.max(s_old, axis=-1, keepdims=True),
                        jnp.max(s_new, axis=-1, keepdims=True))
        p_old = jnp.exp2(s_old - m)
        p_new = jnp.exp2(s_new - m)
        l_s[h] = jnp.sum(p_old, axis=-1, keepdims=True) + jnp.sum(p_new, axis=-1, keepdims=True)
        po_s[h] = p_old.astype(BF16)
        pn_s[h] = p_new.astype(BF16)
    outs = []
    for h in range(nh):
        cv = cv_ref[0, pl.ds(h, past, stride=nh), :].astype(BF16)
        vn = vn_ref[0, pl.ds(h, tq, stride=nh), :].astype(BF16)
        pv = (jnp.dot(po_s[h], cv, preferred_element_type=F32)
              + jnp.dot(pn_s[h], vn, preferred_element_type=F32)) / l_s[h]
        oh = pv[0:tq] - lam * pv[tq:2 * tq]
        ms = jnp.mean(oh * oh, axis=-1, keepdims=True)
        outs.append(oh * lax.rsqrt(ms + EPS) * hg_ref[...] * (1.0 - lam_init))
    o = (jnp.concatenate(outs, axis=1) * sg_ref[0]).astype(BF16)
    o_ref[0] = x_ref[0] + jnp.dot(o, w_out_ref[...], preferred_element_type=F32)


def _sample_attn(q, k_new, v_new, cache_k, cache_v, sg, x, w_out, head_g, lq1, lk1, lq2, lk2, *, lam_init):
    nb, tq, w = q.shape
    past = cache_k.shape[1]
    d = x.shape[-1]
    full = lambda shape: pl.BlockSpec(shape, lambda b: (0,) * len(shape))
    tok = lambda width: pl.BlockSpec((1, tq, width), lambda b: (b, 0, 0))
    nh = w // V_DIM
    old = pl.BlockSpec((1, past * nh, V_DIM), lambda b: (b, 0, 0))
    new = pl.BlockSpec((1, tq * nh, V_DIM), lambda b: (b, 0, 0))
    row = lambda a: a.reshape(1, -1).astype(F32)
    return pl.pallas_call(
        functools.partial(_sample_attn_kernel, lam_init=lam_init),
        grid=(nb,),
        in_specs=[tok(w), new, new, old, old, tok(w), tok(d),
                  full((w, d)), full((1, V_DIM)),
                  full((1, HEAD_DIM)), full((1, HEAD_DIM)), full((1, HEAD_DIM)), full((1, HEAD_DIM))],
        out_specs=tok(d),
        out_shape=jax.ShapeDtypeStruct((nb, tq, d), F32),
        scratch_shapes=[
            pltpu.VMEM((nh, 2 * tq, past), F32), pltpu.VMEM((nh, 2 * tq, tq), F32),
            pltpu.VMEM((nh, 2 * tq, past), BF16), pltpu.VMEM((nh, 2 * tq, tq), BF16),
            pltpu.VMEM((nh, 2 * tq, 1), F32),
        ],
        compiler_params=pltpu.CompilerParams(
            dimension_semantics=("parallel",), vmem_limit_bytes=VMEM_LIMIT_BYTES),
        name="attn_sample",
    )(q, k_new, v_new, cache_k.reshape(nb, past * nh, V_DIM), cache_v.reshape(nb, past * nh, V_DIM), sg, x,
      w_out.astype(BF16), row(head_g), row(lq1), row(lk1), row(lq2), row(lk2))


def _pick_tile(n, target):
    tile = min(n, target)
    while n % tile:
        tile //= 2
    return tile


def kernel(x_prompt, x_sample, state_conv, state_h, cache_k, cache_v, a_norm, a_w_in, a_conv_w, a_conv_b, a_gate_r_w, a_gate_r_b, a_gate_i_w, a_gate_i_b, a_lambda, a_w_out, kv_norm, kv_w, k_norm, b_norm, b_w_in, b_q_norm, b_lambda_q1, b_lambda_k1, b_lambda_q2, b_lambda_k2, b_head_norm, b_w_out):
    n_a, n_b = a_norm.shape[0], b_norm.shape[0]
    assert n_a == 1 and n_b == 1, "one recurrent layer followed by one attention layer"
    nb, t, d = x_prompt.shape
    nbs, ts, _ = x_sample.shape
    d_rnn = a_lambda.shape[-1]
    w = kv_w.shape[1] // 2
    nh = w // V_DIM
    past = cache_k.shape[1]
    halo = CONV_W - 1
    lam_init = 0.8 - 0.6 * math.exp(-0.3 * n_a)
    a_args = (a_norm[0], a_w_in[0], a_conv_w[0], a_conv_b[0], a_gate_r_w[0], a_gate_r_b[0],
              a_gate_i_w[0], a_gate_i_b[0], a_lambda[0], a_w_out[0])
    lam_args = (b_lambda_q1[0], b_lambda_k1[0], b_lambda_q2[0], b_lambda_k2[0])

    def conv_state_out(c, n):
        return jnp.swapaxes(c.reshape(halo, n, d_rnn), 0, 1)[None]

    x1, conv_p, h_p = _rnn_layer(
        x_prompt, jnp.zeros((halo * nb, d_rnn), F32), jnp.zeros((nb, d_rnn), F32), *a_args,
        reset=True, tt=_pick_tile(t, 128))
    tq = _pick_tile(t, 256)
    k_p, v_p, khm, vt, qt, sg = _proj(
        x1, jnp.arange(t, dtype=jnp.int32), kv_norm, b_norm[0], kv_w, b_w_in[0], k_norm, b_q_norm[0],
        prompt=True, rows=_pick_tile(t, 1024), kt=tq)
    y_p = _prompt_attn(qt, khm, vt, sg, x1, b_w_out[0], b_head_norm[0], *lam_args, k_norm,
                       lam_init=lam_init)

    conv0 = jnp.swapaxes(state_conv[0], 0, 1).reshape(halo * nbs, d_rnn)
    x1s, conv_s, h_s = _rnn_layer(x_sample, conv0, state_h[0], *a_args, reset=False, tt=ts)
    k_s, v_s, q_s, sg_s = _proj(
        x1s.reshape(1, nbs * ts, d), jnp.tile(past + jnp.arange(ts, dtype=jnp.int32), nbs),
        kv_norm, b_norm[0], kv_w, b_w_in[0], k_norm, b_q_norm[0], prompt=False, rows=nbs * ts, kt=ts)
    k_s, v_s = k_s.reshape(nbs, ts * nh, V_DIM), v_s.reshape(nbs, ts * nh, V_DIM)
    q_s, sg_s = q_s.reshape(nbs, ts, w), sg_s.reshape(nbs, ts, w)
    y_s = _sample_attn(q_s, k_s, v_s, cache_k, cache_v, sg_s, x1s, b_w_out[0], b_head_norm[0],
                       *lam_args, lam_init=lam_init)

    return (y_p, y_s,
            conv_state_out(conv_p, nb), h_p[None],
            k_p.reshape(nb, t, nh, V_DIM), v_p.reshape(nb, t, nh, V_DIM),
            conv_state_out(conv_s, nbs), h_s[None],
            k_s.reshape(nbs, ts, nh, V_DIM), v_s.reshape(nbs, ts, nh, V_DIM))
```
